```python
import jax, jax.numpy as jnp
from jax import lax
import numpy as np

D_MODEL = 4096
BATCH = 1
SEQ = 16384
DEPTH = 1

ATT_HEADS = 16
ATT_HEAD_DIM = 128
ATT_WIDTH = ATT_HEADS * ATT_HEAD_DIM
Q_BLOCK = 128
RWKV_HEADS = 32
RWKV_HEAD_DIM = 64
RWKV_WIDTH = RWKV_HEADS * RWKV_HEAD_DIM
DECAY_RANK = 128
ICLR_RANK = 128
GATE_RANK = 256
RWKV_GN_EPS = 64e-5
N_BRANCHES = 2
N_EXPERTS = 32
TOP_K = 4
EXPERT_DIM = 1536
SWIGLU_LIMIT = 7.0
SWIGLU_ALPHA = 1.702
ROW_BLOCK = 128
DEEPNORM_ALPHA = (2 * DEPTH) ** 0.25
DEEPNORM_BETA = (8 * DEPTH) ** -0.25
LN_EPS = 1e-5
N_MOD = 6

OFF_Q = 0
OFF_K = OFF_Q + ATT_WIDTH
OFF_V = OFF_K + ATT_WIDTH
OFF_F = OFF_V + ATT_WIDTH
OFF_RWKV = OFF_F + ATT_HEADS
RWKV_COLS = 3 * RWKV_WIDTH + DECAY_RANK + ICLR_RANK + GATE_RANK
OFF_GATE = OFF_RWKV + RWKV_COLS
IN_COLS = OFF_GATE + N_BRANCHES * D_MODEL

kernel_name = 'fox_rwkv7_gated_moe_deepnorm_adaln'


def layer_norm(x, g, b):
    xf = x.astype(jnp.float32)
    mu = jnp.mean(xf, axis=-1, keepdims=True)
    var = jnp.mean(jnp.square(xf - mu), axis=-1, keepdims=True)
    y = (xf - mu) * lax.rsqrt(var + LN_EPS)
    return (y * g.astype(jnp.float32) + b.astype(jnp.float32)).astype(x.dtype)


def fox_attention(q, k, v, f_logit, f_bias):
    B, S, H, Dh = q.shape
    log_f = jax.nn.log_sigmoid(f_logit.astype(jnp.float32) + f_bias.astype(jnp.float32))
    cum = jnp.cumsum(log_f, axis=1)
    cum_k = jnp.transpose(cum, (0, 2, 1))
    nb = S // Q_BLOCK
    q_blocks = q.reshape(B, nb, Q_BLOCK, H, Dh).transpose(1, 0, 2, 3, 4)
    cum_q_blocks = cum_k.reshape(B, H, nb, Q_BLOCK).transpose(2, 0, 1, 3)
    k_pos = jnp.arange(S)
    scale = Dh ** -0.5

    def one_block(args):
        blk, q_blk, cq = args
        s = jnp.einsum('bqhd,bkhd->bhqk', q_blk, k).astype(jnp.float32) * scale
        s = s + cq[..., :, None] - cum_k[:, :, None, :]
        q_pos = blk * Q_BLOCK + jnp.arange(Q_BLOCK)
        causal = k_pos[None, :] <= q_pos[:, None]
        s = jnp.where(causal, s, -jnp.inf)
        p = jax.nn.softmax(s, axis=-1)
        return jnp.einsum('bhqk,bkhd->bqhd', p.astype(v.dtype), v)

    out = lax.map(one_block, (jnp.arange(nb), q_blocks, cum_q_blocks))
    return out.transpose(1, 0, 2, 3, 4).reshape(B, S, H * Dh)


def rwkv7_time_mix(p, mu, w0, w2, a0, a2, g2, k_k, k_a, r_k, gn_g, gn_b):
    B, S, _ = p.shape
    H, N, C = RWKV_HEADS, RWKV_HEAD_DIM, RWKV_WIDTH
    f32 = jnp.float32
    p_prev = jnp.pad(p, ((0, 0), (1, 0), (0, 0)))[:, :S]
    xs = p + (p_prev - p) * mu
    r = xs[..., 0:C]
    k = xs[..., C:2 * C]
    v = xs[..., 2 * C:3 * C]
    wd = xs[..., 3 * C:3 * C + DECAY_RANK]
    ad = xs[..., 3 * C + DECAY_RANK:3 * C + DECAY_RANK + ICLR_RANK]
    gd = xs[..., 3 * C + DECAY_RANK + ICLR_RANK:]
    w_raw = -jax.nn.softplus(-(w0 + jnp.tanh(wd) @ w2).astype(f32)) - 0.5
    decay = jnp.exp(-jnp.exp(w_raw))
    a = jax.nn.sigmoid((a0 + ad @ a2).astype(f32))
    g = jax.nn.sigmoid(gd) @ g2
    kk = (k * k_k).astype(f32).reshape(B, S, H, N)
    kk = kk / jnp.maximum(jnp.sqrt(jnp.sum(kk * kk, axis=-1, keepdims=True)), 1e-12)
    k_mod = k.astype(f32) * (1.0 + (a - 1.0) * k_a.astype(f32))
    r_h = r.astype(f32).reshape(B, S, H, N)
    k_h = k_mod.reshape(B, S, H, N)
    v_h = v.astype(f32).reshape(B, S, H, N)
    w_h = decay.reshape(B, S, H, N)
    a_h = a.reshape(B, S, H, N)

    def step(state, inp):
        r_t, w_t, k_t, v_t, kk_t, a_t = inp
        sa = jnp.einsum('bhij,bhj->bhi', state, -kk_t)
        state = (state * w_t[:, :, None, :]
                 + sa[..., None] * (kk_t * a_t)[:, :, None, :]
                 + v_t[..., None] * k_t[:, :, None, :])
        return state, jnp.einsum('bhij,bhj->bhi', state, r_t)

    state0 = jnp.zeros((B, H, N, N), f32)
    xs_seq = (jnp.moveaxis(r_h, 1, 0), jnp.moveaxis(w_h, 1, 0), jnp.moveaxis(k_h, 1, 0),
              jnp.moveaxis(v_h, 1, 0), jnp.moveaxis(kk, 1, 0), jnp.moveaxis(a_h, 1, 0))
    _, y = lax.scan(step, state0, xs_seq)
    y = jnp.moveaxis(y, 0, 1)
    y_mu = jnp.mean(y, axis=-1, keepdims=True)
    y_var = jnp.mean(jnp.square(y - y_mu), axis=-1, keepdims=True)
    y = ((y - y_mu) * lax.rsqrt(y_var + RWKV_GN_EPS)).reshape(B, S, C)
    y = y * gn_g.astype(f32) + gn_b.astype(f32)
    bonus = jnp.sum(r_h * k_h * r_k.astype(f32), axis=-1, keepdims=True) * v_h
    y = y + bonus.reshape(B, S, C)
    return (y * g.astype(f32)).astype(p.dtype)


def mixer_sublayer(h, w_in, fox_f_bias, rwkv_mu, rwkv_w0, rwkv_w2, rwkv_a0, rwkv_a2, rwkv_g2,
                   rwkv_k_k, rwkv_k_a, rwkv_r_k, rwkv_gn_g, rwkv_gn_b, w_up_att, w_up_rwkv, w_o):
    B, S, _ = h.shape
    proj = h @ w_in
    q = proj[..., OFF_Q:OFF_K].reshape(B, S, ATT_HEADS, ATT_HEAD_DIM)
    k = proj[..., OFF_K:OFF_V].reshape(B, S, ATT_HEADS, ATT_HEAD_DIM)
    v = proj[..., OFF_V:OFF_F].reshape(B, S, ATT_HEADS, ATT_HEAD_DIM)
    att = fox_attention(q, k, v, proj[..., OFF_F:OFF_RWKV], fox_f_bias)
    rw = rwkv7_time_mix(proj[..., OFF_RWKV:OFF_GATE], rwkv_mu, rwkv_w0, rwkv_w2, rwkv_a0, rwkv_a2,
                        rwkv_g2, rwkv_k_k, rwkv_k_a, rwkv_r_k, rwkv_gn_g, rwkv_gn_b)
    gates = jax.nn.sigmoid(proj[..., OFF_GATE:])
    merged = (gates[..., :D_MODEL] * (att @ w_up_att)
              + gates[..., D_MODEL:] * (rw @ w_up_rwkv))
    return merged @ w_o


def moe_ffn(h, router_w, router_b, w_gate_up, b_gate_up, w_down, b_down):
    B, S, D = h.shape
    T = B * S
    hf = h.reshape(T, D)
    logits = (hf @ router_w).astype(jnp.float32) + router_b.astype(jnp.float32)
    top_vals, top_idx = lax.top_k(logits, TOP_K)
    gate_w = jax.nn.softmax(top_vals, axis=-1)
    TK = T * TOP_K
    flat_e = top_idx.reshape(TK)
    flat_tok = jnp.arange(TK, dtype=jnp.int32) // TOP_K
    order = jnp.argsort(flat_e)
    sorted_e = flat_e[order]
    counts = jnp.bincount(flat_e, length=N_EXPERTS)
    padded = ((counts + ROW_BLOCK - 1) // ROW_BLOCK) * ROW_BLOCK
    start = jnp.cumsum(counts) - counts
    pad_end = jnp.cumsum(padded)
    pad_start = pad_end - padded
    dest = pad_start[sorted_e] + (jnp.arange(TK) - start[sorted_e])
    n_blocks = -(-TK // ROW_BLOCK) + N_EXPERTS
    P = n_blocks * ROW_BLOCK
    row_tok = jnp.full((P,), T, jnp.int32).at[dest].set(flat_tok[order])
    row_w = jnp.zeros((P,), jnp.float32).at[dest].set(gate_w.reshape(TK)[order])
    block_e = jnp.minimum(jnp.searchsorted(pad_end, jnp.arange(n_blocks) * ROW_BLOCK, side='right'),
                          N_EXPERTS - 1)
    h_pad = jnp.concatenate([hf, jnp.zeros((1, D), hf.dtype)], axis=0)

    def body(acc, blk):
        e, tok, wt = blk
        xb = h_pad[tok]
        gu = xb @ w_gate_up[e] + b_gate_up[e]
        gate = jnp.minimum(gu[:, :EXPERT_DIM], SWIGLU_LIMIT)
        up = jnp.clip(gu[:, EXPERT_DIM:], -SWIGLU_LIMIT, SWIGLU_LIMIT)
        act = (up + 1.0) * (gate * jax.nn.sigmoid(gate * SWIGLU_ALPHA))
        out = act @ w_down[e] + b_down[e]
        acc = acc.at[tok].add(out.astype(jnp.float32) * wt[:, None])
        return acc, None

    acc0 = jnp.zeros((T + 1, D), jnp.float32)
    acc, _ = lax.scan(body, acc0, (block_e, row_tok.reshape(n_blocks, ROW_BLOCK),
                                   row_w.reshape(n_blocks, ROW_BLOCK)))
    return acc[:T].reshape(B, S, D).astype(h.dtype)


def setup_inputs(seed: int = 0) -> dict:
    key = jax.random.key(seed)
    ks = jax.random.split(key, 32)
    f32 = jnp.float32
    L = DEPTH

    def nrm(k, shape, scale):
        return jax.random.normal(k, shape, f32) * scale

    return {
        'x': nrm(ks[0], (BATCH, SEQ, D_MODEL), 1.0),
        'c': nrm(ks[1], (BATCH, D_MODEL), 1.0),
        'w_ada': nrm(ks[2], (L, D_MODEL, N_MOD * D_MODEL), D_MODEL ** -0.5),
        'b_ada': nrm(ks[3], (L, N_MOD * D_MODEL), 0.02),
        'w_in': nrm(ks[4], (L, D_MODEL, IN_COLS), D_MODEL ** -0.5),
        'fox_f_bias': jax.random.uniform(ks[5], (L, ATT_HEADS), f32, minval=1.0, maxval=5.0),
        'rwkv_mu': jax.random.uniform(ks[6], (L, RWKV_COLS), f32),
        'rwkv_w0': jax.random.uniform(ks[7], (L, RWKV_WIDTH), f32, minval=-6.0, maxval=-1.0),
        'rwkv_w2': nrm(ks[8], (L, DECAY_RANK, RWKV_WIDTH), 0.5 * DECAY_RANK ** -0.5),
        'rwkv_a0': nrm(ks[9], (L, RWKV_WIDTH), 0.1),
        'rwkv_a2': nrm(ks[10], (L, ICLR_RANK, RWKV_WIDTH), ICLR_RANK ** -0.5),
        'rwkv_g2': nrm(ks[11], (L, GATE_RANK, RWKV_WIDTH), GATE_RANK ** -0.5),
        'rwkv_k_k': 0.85 + nrm(ks[12], (L, RWKV_WIDTH), 0.02),
        'rwkv_k_a': 1.0 + nrm(ks[13], (L, RWKV_WIDTH), 0.02),
        'rwkv_r_k': nrm(ks[14], (L, RWKV_HEADS, RWKV_HEAD_DIM), 0.1),
        'rwkv_gn_g': 1.0 + nrm(ks[15], (L, RWKV_WIDTH), 0.02),
        'rwkv_gn_b': nrm(ks[16], (L, RWKV_WIDTH), 0.02),
        'w_up_att': nrm(ks[17], (L, ATT_WIDTH, D_MODEL), ATT_WIDTH ** -0.5),
        'w_up_rwkv': nrm(ks[18], (L, RWKV_WIDTH, D_MODEL), RWKV_WIDTH ** -0.5),
        'w_o': nrm(ks[19], (L, D_MODEL, D_MODEL), DEEPNORM_BETA * D_MODEL ** -0.5),
        'ln1_g': 1.0 + nrm(ks[20], (L, D_MODEL), 0.02),
        'ln1_b': nrm(ks[21], (L, D_MODEL), 0.02),
        'router_w': nrm(ks[22], (L, D_MODEL, N_EXPERTS), D_MODEL ** -0.5),
        'router_b': nrm(ks[23], (L, N_EXPERTS), 0.01),
        'w_gate_up': nrm(ks[24], (L, N_EXPERTS, D_MODEL, 2 * EXPERT_DIM), D_MODEL ** -0.5),
        'b_gate_up': nrm(ks[25], (L, N_EXPERTS, 2 * EXPERT_DIM), 0.02),
        'w_down': nrm(ks[26], (L, N_EXPERTS, EXPERT_DIM, D_MODEL), DEEPNORM_BETA * EXPERT_DIM ** -0.5),
        'b_down': nrm(ks[27], (L, N_EXPERTS, D_MODEL), 0.02),
        'ln2_g': 1.0 + nrm(ks[28], (L, D_MODEL), 0.02),
        'ln2_b': nrm(ks[29], (L, D_MODEL), 0.02),
    }


def reference(x, c, w_ada, b_ada, w_in, fox_f_bias, rwkv_mu, rwkv_w0, rwkv_w2, rwkv_a0, rwkv_a2,
              rwkv_g2, rwkv_k_k, rwkv_k_a, rwkv_r_k, rwkv_gn_g, rwkv_gn_b, w_up_att, w_up_rwkv, w_o,
              ln1_g, ln1_b, router_w, router_b, w_gate_up, b_gate_up, w_down, b_down, ln2_g, ln2_b):
    cond = jax.nn.silu(c)
    for layer in range(DEPTH):
        ada = cond @ w_ada[layer] + b_ada[layer]
        mods = jnp.split(ada, N_MOD, axis=-1)
        shift_m, scale_m, gate_m = mods[0][:, None, :], mods[1][:, None, :], mods[2][:, None, :]
        shift_f, scale_f, gate_f = mods[3][:, None, :], mods[4][:, None, :], mods[5][:, None, :]
        h = x * (1.0 + scale_m) + shift_m
        mix = mixer_sublayer(h, w_in[layer], fox_f_bias[layer], rwkv_mu[layer], rwkv_w0[layer],
                             rwkv_w2[layer], rwkv_a0[layer], rwkv_a2[layer], rwkv_g2[layer],
                             rwkv_k_k[layer], rwkv_k_a[layer], rwkv_r_k[layer], rwkv_gn_g[layer],
                             rwkv_gn_b[layer], w_up_att[layer], w_up_rwkv[layer], w_o[layer])
        x = layer_norm(DEEPNORM_ALPHA * x + gate_m * mix, ln1_g[layer], ln1_b[layer])
        h = x * (1.0 + scale_f) + shift_f
        ff = moe_ffn(h, router_w[layer], router_b[layer], w_gate_up[layer], b_gate_up[layer],
                     w_down[layer], b_down[layer])
        x = layer_norm(DEEPNORM_ALPHA * x + gate_f * ff, ln2_g[layer], ln2_b[layer])
    return x
```

```python
import functools

import jax
import jax.numpy as jnp
from jax import lax
from jax.experimental import pallas as pl
from jax.experimental.pallas import tpu as pltpu

F32 = jnp.float32
BF16 = jnp.bfloat16

TOP_K = 4
SWIGLU_LIMIT = 7.0
SWIGLU_ALPHA = 1.702
LN_EPS = 1e-5
RWKV_GN_EPS = 64e-5

LANES = 128
VMEM_LIMIT_BYTES = 56 * 1024 * 1024
RWKV_CHUNK = 64
NEG_INF = float("-inf")


def _pick(n, pref, align=8):
    if n <= pref:
        return n
    t = (pref // align) * align
    while t >= align:
        if n % t == 0:
            return t
        t -= align
    return n


def _params(sem):
    return pltpu.CompilerParams(dimension_semantics=sem, vmem_limit_bytes=VMEM_LIMIT_BYTES)


def _bdot(a, b):
    return jnp.dot(a.astype(BF16), b.astype(BF16), preferred_element_type=F32)


def _bdot_nt(a, b):
    return lax.dot_general(a.astype(BF16), b.astype(BF16), (((1,), (1,)), ((), ())),
                           preferred_element_type=F32)


def _bdot_tn(a, b):
    return lax.dot_general(a.astype(BF16), b.astype(BF16), (((0,), (0,)), ((), ())),
                           preferred_element_type=F32)


def _split2(x):
    hi = x.astype(BF16)
    lo = (x - hi.astype(F32)).astype(BF16)
    return hi, lo


def _dot_hl(x, m):
    hi, lo = _split2(x)
    return jnp.dot(hi, m, preferred_element_type=F32) + jnp.dot(lo, m, preferred_element_type=F32)


def _dot_lh3(m, x):
    hi = x.astype(BF16)
    r1 = x - hi.astype(F32)
    mid = r1.astype(BF16)
    lo = (r1 - mid.astype(F32)).astype(BF16)
    return (jnp.dot(m, hi, preferred_element_type=F32) + jnp.dot(m, mid, preferred_element_type=F32)
            + jnp.dot(m, lo, preferred_element_type=F32))


def _softplus(u):
    return jnp.maximum(u, 0.0) + jnp.log(1.0 + jnp.exp(-jnp.abs(u)))


def _ada_kernel(c_ref, w_ref, b_ref, o_ref):
    cv = c_ref[...]
    cond = cv * jax.nn.sigmoid(cv)
    o_ref[...] = jnp.sum(w_ref[...] * cond, axis=0, keepdims=True) + b_ref[...]


def _ada(c_col, w, b_row):
    d, n = w.shape
    tn = _pick(n, 512, LANES)
    return pl.pallas_call(
        _ada_kernel,
        grid=(n // tn,),
        in_specs=[pl.BlockSpec((d, 1), lambda j: (0, 0)),
                  pl.BlockSpec((d, tn), lambda j: (0, j)),
                  pl.BlockSpec((1, tn), lambda j: (0, j))],
        out_specs=pl.BlockSpec((1, tn), lambda j: (0, j)),
        out_shape=jax.ShapeDtypeStruct((1, n), F32),
        compiler_params=_params(("parallel",)),
        name="ada_matvec",
    )(c_col, w, b_row)


def _modulate_kernel(x_ref, sc_ref, sh_ref, o_ref):
    o_ref[...] = (x_ref[...] * (1.0 + sc_ref[...]) + sh_ref[...]).astype(o_ref.dtype)


def _modulate(x, scale, shift):
    s, d = x.shape
    tm = _pick(s, 512)
    row = pl.BlockSpec((1, d), lambda i: (0, 0))
    return pl.pallas_call(
        _modulate_kernel,
        grid=(s // tm,),
        in_specs=[pl.BlockSpec((tm, d), lambda i: (i, 0)), row, row],
        out_specs=pl.BlockSpec((tm, d), lambda i: (i, 0)),
        out_shape=jax.ShapeDtypeStruct((s, d), BF16),
        compiler_params=_params(("parallel",)),
        name="modulate",
    )(x, scale, shift)


def _mm_kernel(a_ref, w_ref, *rest, has_scale, act):
    o_ref = rest[-1]
    acc = jnp.dot(a_ref[...], w_ref[...], preferred_element_type=F32)
    if has_scale:
        acc = acc * rest[0][...]
    if act == "sigmoid":
        acc = jax.nn.sigmoid(acc)
    o_ref[...] = acc.astype(o_ref.dtype)


def _matmul(a, w, out_dtype, col_scale=None, act=None, tm_pref=1024, tn_pref=512, name="matmul"):
    m, k = a.shape
    n = w.shape[1]
    tm = _pick(m, tm_pref)
    tn = _pick(n, tn_pref, LANES)
    in_specs = [pl.BlockSpec((tm, k), lambda i, j: (i, 0)),
                pl.BlockSpec((k, tn), lambda i, j: (0, j))]
    args = [a, w]
    if col_scale is not None:
        in_specs.append(pl.BlockSpec((1, tn), lambda i, j: (0, j)))
        args.append(col_scale)
    return pl.pallas_call(
        functools.partial(_mm_kernel, has_scale=col_scale is not None, act=act),
        grid=(m // tm, n // tn),
        in_specs=in_specs,
        out_specs=pl.BlockSpec((tm, tn), lambda i, j: (i, j)),
        out_shape=jax.ShapeDtypeStruct((m, n), out_dtype),
        compiler_params=_params(("parallel", "arbitrary")),
        name=name,
    )(*args)


def _merge_kernel(att_ref, rw_ref, wa_ref, wr_ref, ga_ref, gr_ref, o_ref):
    ya = jnp.dot(att_ref[...], wa_ref[...], preferred_element_type=F32)
    yr = jnp.dot(rw_ref[...], wr_ref[...], preferred_element_type=F32)
    o_ref[...] = (ga_ref[...].astype(F32) * ya + gr_ref[...].astype(F32) * yr).astype(o_ref.dtype)


def _merge(att, rw, w_up_att, w_up_rwkv, gates, d):
    s, ka = att.shape
    kr = rw.shape[1]
    tm = _pick(s, 1024)
    tn = _pick(d, 512, LANES)
    nj = d // tn
    return pl.pallas_call(
        _merge_kernel,
        grid=(s // tm, nj),
        in_specs=[pl.BlockSpec((tm, ka), lambda i, j: (i, 0)),
                  pl.BlockSpec((tm, kr), lambda i, j: (i, 0)),
                  pl.BlockSpec((ka, tn), lambda i, j: (0, j)),
                  pl.BlockSpec((kr, tn), lambda i, j: (0, j)),
                  pl.BlockSpec((tm, tn), lambda i, j: (i, j)),
                  pl.BlockSpec((tm, tn), lambda i, j: (i, j + nj))],
        out_specs=pl.BlockSpec((tm, tn), lambda i, j: (i, j)),
        out_shape=jax.ShapeDtypeStruct((s, d), BF16),
        compiler_params=_params(("parallel", "arbitrary")),
        name="branch_merge",
    )(att, rw, w_up_att, w_up_rwkv, gates, gates)


def _cumsum_kernel(f_ref, b_ref, o_ref, *, n_chunks):
    h = f_ref.shape[0]
    row = lax.broadcasted_iota(jnp.int32, (LANES, LANES), 0)
    col = lax.broadcasted_iota(jnp.int32, (LANES, LANES), 1)
    upper = (row <= col).astype(BF16)
    bias = b_ref[...]

    def body(c, carry):
        off = pl.multiple_of(c * LANES, LANES)
        z = f_ref[:, pl.ds(off, LANES)] + bias
        log_f = -_softplus(-z)
        hi = log_f.astype(BF16)
        r1 = log_f - hi.astype(F32)
        mid = r1.astype(BF16)
        lo = (r1 - mid.astype(F32)).astype(BF16)
        cs = (jnp.dot(hi, upper, preferred_element_type=F32)
              + jnp.dot(mid, upper, preferred_element_type=F32)
              + jnp.dot(lo, upper, preferred_element_type=F32)) + carry
        o_ref[:, pl.ds(off, LANES)] = cs
        return cs[:, LANES - 1:LANES]

    lax.fori_loop(0, n_chunks, body, jnp.zeros((h, 1), F32))


def _forget_cumsum(f_t, bias_col):
    h, s = f_t.shape
    assert s % LANES == 0
    return pl.pallas_call(
        functools.partial(_cumsum_kernel, n_chunks=s // LANES),
        grid=(1,),
        in_specs=[pl.BlockSpec((h, s), lambda i: (0, 0)), pl.BlockSpec((h, 1), lambda i: (0, 0))],
        out_specs=pl.BlockSpec((h, s), lambda i: (0, 0)),
        out_shape=jax.ShapeDtypeStruct((h, s), F32),
        compiler_params=_params(("arbitrary",)),
        name="forget_cumsum",
    )(f_t, bias_col)


def _fox_kernel(q_ref, k_ref, v_ref, ck_ref, cq_ref, o_ref, m_sc, l_sc, acc_sc, *, tq):
    i = pl.program_id(1)
    m_sc[...] = jnp.full(m_sc.shape, NEG_INF, F32)
    l_sc[...] = jnp.zeros(l_sc.shape, F32)
    acc_sc[...] = jnp.zeros(acc_sc.shape, F32)
    q = q_ref[...]
    cq = cq_ref[...]

    def step(j, masked):
        off = pl.multiple_of(j * tq, tq)
        k = k_ref[pl.ds(off, tq), :]
        v = v_ref[pl.ds(off, tq), :]
        s = lax.dot_general(q, k, (((1,), (1,)), ((), ())), preferred_element_type=F32)
        t = s - ck_ref[:, pl.ds(off, tq)]
        if masked:
            r = lax.broadcasted_iota(jnp.int32, (tq, tq), 0)
            c = lax.broadcasted_iota(jnp.int32, (tq, tq), 1)
            t = jnp.where(c <= r, t, NEG_INF)
        m_prev = m_sc[...]
        m_new = jnp.maximum(m_prev, jnp.max(t, axis=1, keepdims=True) + cq)
        alpha = jnp.exp(m_prev - m_new)
        p = jnp.exp(t - (m_new - cq))
        l_sc[...] = alpha * l_sc[...] + jnp.sum(p, axis=1, keepdims=True)
        acc_sc[...] = alpha * acc_sc[...] + jnp.dot(p.astype(v.dtype), v, preferred_element_type=F32)
        m_sc[...] = m_new

    def body(j, carry):
        step(j, False)
        return carry

    lax.fori_loop(0, i, body, 0)
    step(i, True)
    o_ref[...] = (acc_sc[...] / l_sc[...]).astype(o_ref.dtype)


def _fox_attention(qkv, cum_k, cum_q, n_heads, dh):
    s = qkv.shape[0]
    tq = _pick(s, 512)
    nq = s // tq
    return pl.pallas_call(
        functools.partial(_fox_kernel, tq=tq),
        grid=(n_heads, nq),
        in_specs=[pl.BlockSpec((tq, dh), lambda h, i: (i, h)),
                  pl.BlockSpec((s, dh), lambda h, i: (0, n_heads + h)),
                  pl.BlockSpec((s, dh), lambda h, i: (0, 2 * n_heads + h)),
                  pl.BlockSpec((None, 1, s), lambda h, i: (h, 0, 0)),
                  pl.BlockSpec((None, tq, 1), lambda h, i: (h, i, 0))],
        out_specs=pl.BlockSpec((tq, dh), lambda h, i: (i, h)),
        out_shape=jax.ShapeDtypeStruct((s, n_heads * dh), BF16),
        scratch_shapes=[pltpu.VMEM((tq, 1), F32), pltpu.VMEM((tq, 1), F32), pltpu.VMEM((tq, dh), F32)],
        compiler_params=_params(("parallel", "arbitrary")),
        name="fox_attention",
    )(qkv, qkv, qkv, cum_k, cum_q)


def _shifted(p_ref, pp_ref, mu_ref, first):
    p = p_ref[...]
    rows = p.shape[0]
    prev_last = jnp.where(first, 0.0, pp_ref[7:8, :])
    rolled = pltpu.roll(p, 1, 0)
    r = lax.broadcasted_iota(jnp.int32, (rows, 1), 0)
    p_prev = jnp.where(r == 0, prev_last, rolled)
    return p + (p_prev - p) * mu_ref[...]


INV_LEAF = 8


def _unit_lower_inverse(lm, sr, sc, n, eye):
    same = lambda b: (sr // b) == (sc // b)
    leaf = min(INV_LEAF, n)
    l0 = jnp.where(same(leaf), lm, 0.0)
    inv = eye - l0
    pw = l0
    span = 2
    while span < leaf:
        pw = _bdot(pw, pw)
        inv = inv + _bdot(inv, pw)
        span *= 2
    b = leaf
    while b < n:
        off = jnp.where(same(2 * b) & jnp.logical_not(same(b)), lm, 0.0)
        inv = inv - _bdot(inv, _bdot(off, inv))
        b *= 2
    return inv


def _rwkv_kernel(pr_ref, pk_ref, pv_ref, pl_ref, ppr_ref, ppk_ref, ppv_ref, ppl_ref,
                 mur_ref, muk_ref, muv_ref, mul_ref,
                 w0_ref, w2_ref, a0_ref, a2_ref, g2_ref, kk_ref, ka_ref, rk_ref, gng_ref, gnb_ref,
                 o_ref, state_sc, *, chunk, n_chunks, head_dim, dr, ir):
    i = pl.program_id(1)
    first = i == 0

    @pl.when(first)
    def _():
        state_sc[...] = jnp.zeros(state_sc.shape, F32)

    r = _shifted(pr_ref, ppr_ref, mur_ref, first)
    k = _shifted(pk_ref, ppk_ref, muk_ref, first)
    v = _shifted(pv_ref, ppv_ref, muv_ref, first)
    low = _shifted(pl_ref, ppl_ref, mul_ref, first)
    wd = low[:, :dr]
    ad = low[:, dr:dr + ir]
    gd = low[:, dr + ir:]

    z = w0_ref[...] + _bdot(jnp.tanh(wd), w2_ref[...])
    log_w = -jnp.exp(-_softplus(-z) - 0.5)
    a = jax.nn.sigmoid(a0_ref[...] + _bdot(ad, a2_ref[...]))
    g = _bdot(jax.nn.sigmoid(gd), g2_ref[...])

    lane_r = lax.broadcasted_iota(jnp.int32, (LANES, LANES), 0)
    lane_c = lax.broadcasted_iota(jnp.int32, (LANES, LANES), 1)
    same_head = (lane_r // head_dim) == (lane_c // head_dim)
    ones_bd = same_head.astype(BF16)

    kk_raw = k * kk_ref[...]
    ss = _dot_hl(kk_raw * kk_raw, ones_bd)
    kappa = kk_raw / jnp.maximum(jnp.sqrt(ss), 1e-12)
    k_mod = k * (1.0 + (a - 1.0) * ka_ref[...])
    beta = kappa * a

    c2 = 2 * chunk
    lane = lax.broadcasted_iota(jnp.int32, (1, LANES), 1)
    head0 = lane < head_dim
    tr = lax.broadcasted_iota(jnp.int32, (chunk, chunk), 0)
    tc = lax.broadcasted_iota(jnp.int32, (chunk, chunk), 1)
    tri = (tc <= tr).astype(BF16)
    sr = lax.broadcasted_iota(jnp.int32, (c2, c2), 0)
    sc = lax.broadcasted_iota(jnp.int32, (c2, c2), 1)
    same_blk = (sr // chunk) == (sc // chunk)
    strict = same_blk & ((sc % chunk) < (sr % chunk))
    incl = same_blk & ((sc % chunk) <= (sr % chunk))
    eye = (sr == sc).astype(F32)

    def stack(x):
        return jnp.concatenate([jnp.where(head0, x, 0.0), jnp.where(head0, 0.0, x)], axis=0)

    ys = []
    state = state_sc[...]
    for c in range(n_chunks):
        sl = slice(c * chunk, (c + 1) * chunk)
        lw = log_w[sl]
        cum = _dot_lh3(tri, lw)
        e_inc = jnp.exp(cum)
        e_neg = jnp.exp(-cum)
        e_exc = jnp.exp(cum - lw)
        rs = stack(r[sl] * e_inc)
        ks = stack(k_mod[sl] * e_neg)
        bs = stack(beta[sl] * e_neg)
        kps = stack(kappa[sl] * e_exc)
        vs = stack(v[sl])
        a_kk = jnp.where(strict, _bdot_nt(kps, ks), 0.0)
        lm = jnp.where(strict, _bdot_nt(kps, bs), 0.0)
        a_rk = jnp.where(incl, _bdot_nt(rs, ks), 0.0)
        a_rb = jnp.where(incl, _bdot_nt(rs, bs), 0.0)
        tinv = _unit_lower_inverse(lm, sr, sc, chunk, eye)
        rhs = _bdot_nt(kps, state) + _bdot(a_kk, vs)
        u = _bdot(tinv, rhs)
        y_st = _bdot_nt(rs, state) + _bdot(a_rk, vs) - _bdot(a_rb, u)
        ys.append(y_st[:chunk] + y_st[chunk:])
        state = (state + _bdot_tn(vs, ks) - _bdot_tn(u, bs)) * e_inc[chunk - 1:chunk, :]
    state_sc[...] = state
    y = jnp.concatenate(ys, axis=0) if n_chunks > 1 else ys[0]

    avg_bd = ones_bd * (1.0 / head_dim)
    mean = _dot_hl(y, avg_bd)
    dlt = y - mean
    var = _dot_hl(dlt * dlt, avg_bd)
    yn = dlt * lax.rsqrt(var + RWKV_GN_EPS) * gng_ref[...] + gnb_ref[...]
    bonus = _dot_hl(r * k_mod * rk_ref[...], ones_bd) * v
    o_ref[...] = ((yn + bonus) * g).astype(o_ref.dtype)


def _rwkv_mix(p, mu, w0, w2, a0, a2, g2, k_k, k_a, r_k, gn_g, gn_b, head_dim):
    s, rc = p.shape
    cw = w0.shape[1]
    dr, ir, gr = w2.shape[0], a2.shape[0], g2.shape[0]
    lw = dr + ir + gr
    assert head_dim * 2 == LANES and cw % LANES == 0 and (head_dim & (head_dim - 1)) == 0
    assert (3 * cw) % lw == 0 and rc == 3 * cw + lw
    chunk = min(RWKV_CHUNK, s)
    tm = _pick(s, 4 * chunk, chunk)
    n_chunks = tm // chunk
    npair = cw // LANES
    low_blk = (3 * cw) // lw
    t8 = tm // 8

    def prev(col_fn):
        return lambda h, i: (jnp.maximum(i * t8 - 1, 0), col_fn(h))

    main = lambda off: pl.BlockSpec((tm, LANES), lambda h, i: (i, off + h))
    prv = lambda off: pl.BlockSpec((8, LANES), prev(lambda h: off + h))
    vec = lambda off: pl.BlockSpec((1, LANES), lambda h, i: (0, off + h))
    in_specs = [
        main(0), main(npair), main(2 * npair), pl.BlockSpec((tm, lw), lambda h, i: (i, low_blk)),
        prv(0), prv(npair), prv(2 * npair), pl.BlockSpec((8, lw), prev(lambda h: low_blk)),
        vec(0), vec(npair), vec(2 * npair), pl.BlockSpec((1, lw), lambda h, i: (0, low_blk)),
        vec(0), pl.BlockSpec((dr, LANES), lambda h, i: (0, h)),
        vec(0), pl.BlockSpec((ir, LANES), lambda h, i: (0, h)),
        pl.BlockSpec((gr, LANES), lambda h, i: (0, h)),
        vec(0), vec(0), vec(0), vec(0), vec(0),
    ]
    return pl.pallas_call(
        functools.partial(_rwkv_kernel, chunk=chunk, n_chunks=n_chunks, head_dim=head_dim, dr=dr, ir=ir),
        grid=(npair, s // tm),
        in_specs=in_specs,
        out_specs=pl.BlockSpec((tm, LANES), lambda h, i: (i, h)),
        out_shape=jax.ShapeDtypeStruct((s, cw), BF16),
        scratch_shapes=[pltpu.VMEM((LANES, LANES), F32)],
        compiler_params=_params(("parallel", "arbitrary")),
        name="rwkv7_mix",
    )(p, p, p, p, p, p, p, p, mu, mu, mu, mu, w0, w2, a0, a2, g2, k_k, k_a, r_k, gn_g, gn_b)


def _layer_norm_rows(z, g, b):
    mu = jnp.mean(z, axis=-1, keepdims=True)
    zc = z - mu
    var = jnp.mean(zc * zc, axis=-1, keepdims=True)
    return zc * lax.rsqrt(var + LN_EPS) * g + b


def _ln1_kernel(x_ref, mix_ref, gate_ref, g_ref, b_ref, sc_ref, sh_ref, rwh_ref, rwl_ref, rb_ref,
                x1_ref, h_ref, lg_ref, *, alpha):
    z = alpha * x_ref[...] + gate_ref[...] * mix_ref[...].astype(F32)
    x1 = _layer_norm_rows(z, g_ref[...], b_ref[...])
    x1_ref[...] = x1
    h = x1 * (1.0 + sc_ref[...]) + sh_ref[...]
    h_ref[...] = h
    hi, lo = _split2(h)
    wh = rwh_ref[...]
    wl = rwl_ref[...]
    lg_ref[...] = (jnp.dot(hi, wh, preferred_element_type=F32) + jnp.dot(lo, wh, preferred_element_type=F32)
                   + jnp.dot(hi, wl, preferred_element_type=F32)) + rb_ref[...]


def _ln1_router(x, mix, gate, g, b, scale, shift, rw_hi, rw_lo, rb, alpha):
    s, d = x.shape
    ne = rw_hi.shape[1]
    tm = _pick(s, 256)
    row = pl.BlockSpec((1, d), lambda i: (0, 0))
    tile = pl.BlockSpec((tm, d), lambda i: (i, 0))
    wspec = pl.BlockSpec((d, ne), lambda i: (0, 0))
    return pl.pallas_call(
        functools.partial(_ln1_kernel, alpha=alpha),
        grid=(s // tm,),
        in_specs=[tile, tile, row, row, row, row, row, wspec, wspec, pl.BlockSpec((1, ne), lambda i: (0, 0))],
        out_specs=[tile, tile, pl.BlockSpec((tm, ne), lambda i: (i, 0))],
        out_shape=[jax.ShapeDtypeStruct((s, d), F32), jax.ShapeDtypeStruct((s, d), F32),
                   jax.ShapeDtypeStruct((s, ne), F32)],
        compiler_params=_params(("parallel",)),
        name="ln1_router",
    )(x, mix, gate, g, b, scale, shift, rw_hi, rw_lo, rb)


def _gather_kernel(tok_ref, h_hbm, o_ref, buf, sem, *, tm):
    base = pl.program_id(0) * tm

    def row_copy(r, tok):
        return pltpu.make_async_copy(h_hbm.at[pl.ds(tok, 1)], buf.at[pl.ds(r, 1)], sem)

    def start(r, carry):
        row_copy(r, tok_ref[base + r]).start()
        return carry

    def wait(r, carry):
        row_copy(r, 0).wait()
        return carry

    lax.fori_loop(0, tm, start, 0)
    lax.fori_loop(0, tm, wait, 0)
    o_ref[...] = buf[...].astype(o_ref.dtype)


def _gather_rows(h, row_tok, tm):
    p = row_tok.shape[0]
    d = h.shape[1]
    return pl.pallas_call(
        functools.partial(_gather_kernel, tm=tm),
        grid_spec=pltpu.PrefetchScalarGridSpec(
            num_scalar_prefetch=1,
            grid=(p // tm,),
            in_specs=[pl.BlockSpec(memory_space=pl.ANY)],
            out_specs=pl.BlockSpec((tm, d), lambda b, tok: (b, 0)),
            scratch_shapes=[pltpu.VMEM((tm, d), h.dtype), pltpu.SemaphoreType.DMA(())]),
        out_shape=jax.ShapeDtypeStruct((p, d), BF16),
        compiler_params=_params(("arbitrary",)),
        name="moe_gather",
    )(row_tok, h)


def _expert_kernel(be_ref, na_ref, x_ref, wg_ref, wu_ref, bg_ref, bu_ref, wd_ref, bd_ref, o_ref, *, nf):
    b = pl.program_id(0)
    f = pl.program_id(1)
    active = b < na_ref[0]

    @pl.when(f == 0)
    def _():
        o_ref[...] = jnp.zeros(o_ref.shape, F32)

    @pl.when(active)
    def _():
        x = x_ref[...]
        gate = jnp.dot(x, wg_ref[...], preferred_element_type=F32) + bg_ref[...]
        up = jnp.dot(x, wu_ref[...], preferred_element_type=F32) + bu_ref[...]
        gate = jnp.minimum(gate, SWIGLU_LIMIT)
        up = jnp.clip(up, -SWIGLU_LIMIT, SWIGLU_LIMIT)
        act = (up + 1.0) * (gate * jax.nn.sigmoid(gate * SWIGLU_ALPHA))
        o_ref[...] += jnp.dot(act.astype(BF16), wd_ref[...], preferred_element_type=F32)

    @pl.when(active & (f == nf - 1))
    def _():
        o_ref[...] += bd_ref[...]


def _expert_ffn(xg, block_e, n_active, w_gu, b_gu, w_down, b_down, tm):
    p, d = xg.shape
    ne, fdim = w_down.shape[0], w_down.shape[1]
    tf = _pick(fdim, 256, LANES)
    nf = fdim // tf

    def fidx(b, f, na):
        return jnp.where(b < na[0], f, nf - 1)

    in_specs = [
        pl.BlockSpec((tm, d), lambda b, f, be, na: (b, 0)),
        pl.BlockSpec((None, d, tf), lambda b, f, be, na: (be[b], 0, fidx(b, f, na))),
        pl.BlockSpec((None, d, tf), lambda b, f, be, na: (be[b], 0, nf + fidx(b, f, na))),
        pl.BlockSpec((None, 1, tf), lambda b, f, be, na: (be[b], 0, fidx(b, f, na))),
        pl.BlockSpec((None, 1, tf), lambda b, f, be, na: (be[b], 0, nf + fidx(b, f, na))),
        pl.BlockSpec((None, tf, d), lambda b, f, be, na: (be[b], fidx(b, f, na), 0)),
        pl.BlockSpec((None, 1, d), lambda b, f, be, na: (be[b], 0, 0)),
    ]
    return pl.pallas_call(
        functools.partial(_expert_kernel, nf=nf),
        grid_spec=pltpu.PrefetchScalarGridSpec(
            num_scalar_prefetch=2,
            grid=(p // tm, nf),
            in_specs=in_specs,
            out_specs=pl.BlockSpec((tm, d), lambda b, f, be, na: (b, 0))),
        out_shape=jax.ShapeDtypeStruct((p, d), F32),
        compiler_params=_params(("arbitrary", "arbitrary")),
        name="moe_experts",
    )(block_e, n_active, xg, w_gu, w_gu, b_gu, b_gu, w_down, b_down)


def _combine_kernel(pos_ref, y_hbm, x1_ref, gw_ref, gate_ref, g_ref, b_ref, o_ref, buf, sem, *, tm, alpha):
    base = pl.program_id(0) * tm * TOP_K

    def row_copy(r, kk, src):
        return pltpu.make_async_copy(y_hbm.at[pl.ds(src, 1)], buf.at[kk, pl.ds(r, 1)], sem)

    def start(r, carry):
        for kk in range(TOP_K):
            row_copy(r, kk, pos_ref[base + r * TOP_K + kk]).start()
        return carry

    def wait(r, carry):
        for kk in range(TOP_K):
            row_copy(r, kk, 0).wait()
        return carry

    lax.fori_loop(0, tm, start, 0)
    lax.fori_loop(0, tm, wait, 0)
    gw = gw_ref[...]
    ff = buf[0] * gw[:, 0:1]
    for kk in range(1, TOP_K):
        ff = ff + buf[kk] * gw[:, kk:kk + 1]
    z = alpha * x1_ref[...] + gate_ref[...] * ff
    o_ref[...] = _layer_norm_rows(z, g_ref[...], b_ref[...])


def _combine_ln2(y, pos, x1, gate_w, gate, g, b, alpha):
    s, d = x1.shape
    tm = _pick(s, 128)
    row = pl.BlockSpec((1, d), lambda i, pos: (0, 0))
    tile = pl.BlockSpec((tm, d), lambda i, pos: (i, 0))
    return pl.pallas_call(
        functools.partial(_combine_kernel, tm=tm, alpha=alpha),
        grid_spec=pltpu.PrefetchScalarGridSpec(
            num_scalar_prefetch=1,
            grid=(s // tm,),
            in_specs=[pl.BlockSpec(memory_space=pl.ANY), tile,
                      pl.BlockSpec((tm, TOP_K), lambda i, pos: (i, 0)), row, row, row],
            out_specs=tile,
            scratch_shapes=[pltpu.VMEM((TOP_K, tm, d), F32), pltpu.SemaphoreType.DMA(())]),
        out_shape=jax.ShapeDtypeStruct((s, d), F32),
        compiler_params=_params(("arbitrary",)),
        name="moe_combine_ln2",
    )(pos, y, x1, gate_w, gate, g, b)


def _routing_plan(logits, n_experts, tm):
    t = logits.shape[0]
    top_vals, top_idx = lax.top_k(logits, TOP_K)
    gate_w = jax.nn.softmax(top_vals, axis=-1)
    tk = t * TOP_K
    flat_e = top_idx.reshape(tk).astype(jnp.int32)
    order = jnp.argsort(flat_e).astype(jnp.int32)
    sorted_e = flat_e[order]
    counts = jnp.bincount(flat_e, length=n_experts).astype(jnp.int32)
    padded = ((counts + tm - 1) // tm) * tm
    start = jnp.cumsum(counts) - counts
    pad_end = jnp.cumsum(padded)
    pad_start = pad_end - padded
    dest = (pad_start[sorted_e] + (jnp.arange(tk, dtype=jnp.int32) - start[sorted_e])).astype(jnp.int32)
    n_blocks = -(-tk // tm) + n_experts
    row_tok = jnp.zeros((n_blocks * tm,), jnp.int32).at[dest].set(order // TOP_K)
    pos = jnp.zeros((tk,), jnp.int32).at[order].set(dest)
    block_e = jnp.minimum(jnp.searchsorted(pad_end, jnp.arange(n_blocks, dtype=jnp.int32) * tm, side="right"),
                          n_experts - 1).astype(jnp.int32)
    n_active = (pad_end[-1:] // tm).astype(jnp.int32)
    return gate_w, row_tok, pos, block_e, n_active


def _layer(x, c_col, prm):
    s, d = x.shape
    n_att_heads = prm["fox_f_bias"].shape[0]
    att_w = prm["w_up_att"].shape[0]
    dh = att_w // n_att_heads
    n_rw_heads, rw_hd = prm["rwkv_r_k"].shape
    rc = prm["rwkv_mu"].shape[0]
    n_experts = prm["router_w"].shape[1]
    depth_alpha = prm["alpha"]

    ada = _ada(c_col, prm["w_ada"], prm["b_ada"][None, :])
    mods = [ada[:, m * d:(m + 1) * d] for m in range(6)]
    shift_m, scale_m, gate_m, shift_f, scale_f, gate_f = mods

    h = _modulate(x, scale_m, shift_m)
    w_in = prm["w_in"]
    off_f = 3 * att_w
    off_rw = off_f + n_att_heads
    off_gate = off_rw + rc
    w_qkv = w_in[:, :off_f].astype(BF16)
    w_f = jnp.pad(w_in[:, off_f:off_rw], ((0, 0), (0, LANES - n_att_heads))).astype(BF16)
    w_rw = w_in[:, off_rw:off_gate].astype(BF16)
    w_gt = w_in[:, off_gate:].astype(BF16)
    q_scale = jnp.concatenate([jnp.full((1, att_w), dh ** -0.5, F32), jnp.ones((1, 2 * att_w), F32)], axis=1)
    qkv = _matmul(h, w_qkv, BF16, col_scale=q_scale, name="proj_qkv")
    f_logit = _matmul(h, w_f, F32, name="proj_forget")
    p_rw = _matmul(h, w_rw, F32, name="proj_rwkv")
    gates = _matmul(h, w_gt, BF16, act="sigmoid", name="proj_gates")

    cum = _forget_cumsum(f_logit[:, :n_att_heads].T, prm["fox_f_bias"][:, None])
    att = _fox_attention(qkv, cum[:, None, :], cum[:, :, None], n_att_heads, dh)

    row = lambda v: v.reshape(1, -1)
    rw = _rwkv_mix(p_rw, row(prm["rwkv_mu"]), row(prm["rwkv_w0"]), prm["rwkv_w2"].astype(BF16),
                   row(prm["rwkv_a0"]), prm["rwkv_a2"].astype(BF16), prm["rwkv_g2"].astype(BF16),
                   row(prm["rwkv_k_k"]), row(prm["rwkv_k_a"]), row(prm["rwkv_r_k"]),
                   row(prm["rwkv_gn_g"]), row(prm["rwkv_gn_b"]), rw_hd)

    merged = _merge(att, rw, prm["w_up_att"].astype(BF16), prm["w_up_rwkv"].astype(BF16), gates, d)
    mix = _matmul(merged, prm["w_o"].astype(BF16), F32, name="proj_out")

    ne_pad = -(-n_experts // LANES) * LANES
    rw_full = jnp.pad(prm["router_w"], ((0, 0), (0, ne_pad - n_experts)))
    rw_hi = rw_full.astype(BF16)
    rw_lo = (rw_full - rw_hi.astype(F32)).astype(BF16)
    rb = jnp.pad(prm["router_b"], (0, ne_pad - n_experts))[None, :]
    x1, h2, logits = _ln1_router(x, mix, gate_m, row(prm["ln1_g"]), row(prm["ln1_b"]), scale_f, shift_f,
                                 rw_hi, rw_lo, rb, depth_alpha)

    tm_e = _pick(s, 512)
    gate_w, row_tok, pos, block_e, n_active = _routing_plan(logits[:, :n_experts], n_experts, tm_e)
    xg = _gather_rows(h2, row_tok, tm_e)
    y = _expert_ffn(xg, block_e, n_active, prm["w_gate_up"].astype(BF16), prm["b_gate_up"][:, None, :],
                    prm["w_down"].astype(BF16), prm["b_down"][:, None, :], tm_e)
    return _combine_ln2(y, pos, x1, gate_w, gate_f, row(prm["ln2_g"]), row(prm["ln2_b"]), depth_alpha)


def kernel(x, c, w_ada, b_ada, w_in, fox_f_bias, rwkv_mu, rwkv_w0, rwkv_w2, rwkv_a0, rwkv_a2, rwkv_g2,
           rwkv_k_k, rwkv_k_a, rwkv_r_k, rwkv_gn_g, rwkv_gn_b, w_up_att, w_up_rwkv, w_o, ln1_g, ln1_b,
           router_w, router_b, w_gate_up, b_gate_up, w_down, b_down, ln2_g, ln2_b):
    stacked = dict(w_ada=w_ada, b_ada=b_ada, w_in=w_in, fox_f_bias=fox_f_bias, rwkv_mu=rwkv_mu,
                   rwkv_w0=rwkv_w0, rwkv_w2=rwkv_w2, rwkv_a0=rwkv_a0, rwkv_a2=rwkv_a2, rwkv_g2=rwkv_g2,
                   rwkv_k_k=rwkv_k_k, rwkv_k_a=rwkv_k_a, rwkv_r_k=rwkv_r_k, rwkv_gn_g=rwkv_gn_g,
                   rwkv_gn_b=rwkv_gn_b, w_up_att=w_up_att, w_up_rwkv=w_up_rwkv, w_o=w_o, ln1_g=ln1_g,
                   ln1_b=ln1_b, router_w=router_w, router_b=router_b, w_gate_up=w_gate_up,
                   b_gate_up=b_gate_up, w_down=w_down, b_down=b_down, ln2_g=ln2_g, ln2_b=ln2_b)
    depth = w_ada.shape[0]
    alpha = (2 * depth) ** 0.25
    outs = []
    for bi in range(x.shape[0]):
        xb = x[bi]
        c_col = c[bi][:, None]
        for layer in range(depth):
            prm = {name: val[layer] for name, val in stacked.items()}
            prm["alpha"] = alpha
            xb = _layer(xb, c_col, prm)
        outs.append(xb)
    return jnp.stack(outs, axis=0)
```

```python
import functools

import jax
import jax.numpy as jnp
from jax import lax
from jax.experimental import pallas as pl
from jax.experimental.pallas import tpu as pltpu

F32 = jnp.float32
BF16 = jnp.bfloat16

TOP_K = 4
SWIGLU_LIMIT = 7.0
SWIGLU_ALPHA = 1.702
LN_EPS = 1e-5
RWKV_GN_EPS = 64e-5

LANES = 128
SUBLANES = 8
VMEM_LIMIT_BYTES = 56 * 1024 * 1024
RWKV_CHUNK = 64
RWKV_TILE_CHUNKS = 16
NEG_INF = float("-inf")


def _pick(n, pref, align=8):
    if n <= pref:
        return n
    t = (pref // align) * align
    while t >= align:
        if n % t == 0:
            return t
        t -= align
    return n


def _params(sem):
    return pltpu.CompilerParams(dimension_semantics=sem, vmem_limit_bytes=VMEM_LIMIT_BYTES)


def _bdot(a, b):
    return jnp.dot(a.astype(BF16), b.astype(BF16), preferred_element_type=F32)


def _bdot_nt(a, b):
    return lax.dot_general(a.astype(BF16), b.astype(BF16), (((1,), (1,)), ((), ())),
                           preferred_element_type=F32)


def _bdot_tn(a, b):
    return lax.dot_general(a.astype(BF16), b.astype(BF16), (((0,), (0,)), ((), ())),
                           preferred_element_type=F32)


def _split2(x):
    hi = x.astype(BF16)
    lo = (x - hi.astype(F32)).astype(BF16)
    return hi, lo


def _dot_hl(x, m):
    hi, lo = _split2(x)
    return jnp.dot(hi, m, preferred_element_type=F32) + jnp.dot(lo, m, preferred_element_type=F32)


def _dot_lh3(m, x):
    hi = x.astype(BF16)
    r1 = x - hi.astype(F32)
    mid = r1.astype(BF16)
    lo = (r1 - mid.astype(F32)).astype(BF16)
    return (jnp.dot(m, hi, preferred_element_type=F32) + jnp.dot(m, mid, preferred_element_type=F32)
            + jnp.dot(m, lo, preferred_element_type=F32))


def _softplus(u):
    return jnp.maximum(u, 0.0) + jnp.log(1.0 + jnp.exp(-jnp.abs(u)))


def _ada_kernel(c_ref, w_ref, b_ref, o_ref):
    cv = c_ref[...]
    cond = cv * jax.nn.sigmoid(cv)
    o_ref[...] = jnp.sum(w_ref[...] * cond, axis=0, keepdims=True) + b_ref[...]


def _ada(c_col, w, b_row):
    d, n = w.shape
    tn = _pick(n, 512, LANES)
    return pl.pallas_call(
        _ada_kernel,
        grid=(n // tn,),
        in_specs=[pl.BlockSpec((d, 1), lambda j: (0, 0)),
                  pl.BlockSpec((d, tn), lambda j: (0, j)),
                  pl.BlockSpec((1, tn), lambda j: (0, j))],
        out_specs=pl.BlockSpec((1, tn), lambda j: (0, j)),
        out_shape=jax.ShapeDtypeStruct((1, n), F32),
        compiler_params=_params(("parallel",)),
        name="ada_matvec",
    )(c_col, w, b_row)


def _modulate_kernel(x_ref, sc_ref, sh_ref, o_ref):
    o_ref[...] = (x_ref[...] * (1.0 + sc_ref[...]) + sh_ref[...]).astype(o_ref.dtype)


def _modulate(x, scale, shift):
    s, d = x.shape
    tm = _pick(s, 512)
    row = pl.BlockSpec((1, d), lambda i: (0, 0))
    return pl.pallas_call(
        _modulate_kernel,
        grid=(s // tm,),
        in_specs=[pl.BlockSpec((tm, d), lambda i: (i, 0)), row, row],
        out_specs=pl.BlockSpec((tm, d), lambda i: (i, 0)),
        out_shape=jax.ShapeDtypeStruct((s, d), BF16),
        compiler_params=_params(("parallel",)),
        name="modulate",
    )(x, scale, shift)


def _mm_kernel(a_ref, w_ref, *rest, has_scale, act):
    o_ref = rest[-1]
    acc = jnp.dot(a_ref[...], w_ref[...], preferred_element_type=F32)
    if has_scale:
        acc = acc * rest[0][...]
    if act == "sigmoid":
        acc = jax.nn.sigmoid(acc)
    o_ref[...] = acc.astype(o_ref.dtype)


def _matmul(a, w, out_dtype, col_scale=None, act=None, tm_pref=1024, tn_pref=512, name="matmul"):
    m, k = a.shape
    n = w.shape[1]
    tm = _pick(m, tm_pref)
    tn = _pick(n, tn_pref, LANES)
    in_specs = [pl.BlockSpec((tm, k), lambda i, j: (i, 0)),
                pl.BlockSpec((k, tn), lambda i, j: (0, j))]
    args = [a, w]
    if col_scale is not None:
        in_specs.append(pl.BlockSpec((1, tn), lambda i, j: (0, j)))
        args.append(col_scale)
    return pl.pallas_call(
        functools.partial(_mm_kernel, has_scale=col_scale is not None, act=act),
        grid=(m // tm, n // tn),
        in_specs=in_specs,
        out_specs=pl.BlockSpec((tm, tn), lambda i, j: (i, j)),
        out_shape=jax.ShapeDtypeStruct((m, n), out_dtype),
        compiler_params=_params(("parallel", "arbitrary")),
        name=name,
    )(*args)


def _merge_kernel(att_ref, rw_ref, wa_ref, wr_ref, ga_ref, gr_ref, o_ref):
    ya = jnp.dot(att_ref[...], wa_ref[...], preferred_element_type=F32)
    yr = jnp.dot(rw_ref[...], wr_ref[...], preferred_element_type=F32)
    o_ref[...] = (ga_ref[...].astype(F32) * ya + gr_ref[...].astype(F32) * yr).astype(o_ref.dtype)


def _merge(att, rw, w_up_att, w_up_rwkv, gates, d):
    s, ka = att.shape
    kr = rw.shape[1]
    tm = _pick(s, 1024)
    tn = _pick(d, 512, LANES)
    nj = d // tn
    return pl.pallas_call(
        _merge_kernel,
        grid=(s // tm, nj),
        in_specs=[pl.BlockSpec((tm, ka), lambda i, j: (i, 0)),
                  pl.BlockSpec((tm, kr), lambda i, j: (i, 0)),
                  pl.BlockSpec((ka, tn), lambda i, j: (0, j)),
                  pl.BlockSpec((kr, tn), lambda i, j: (0, j)),
                  pl.BlockSpec((tm, tn), lambda i, j: (i, j)),
                  pl.BlockSpec((tm, tn), lambda i, j: (i, j + nj))],
        out_specs=pl.BlockSpec((tm, tn), lambda i, j: (i, j)),
        out_shape=jax.ShapeDtypeStruct((s, d), BF16),
        compiler_params=_params(("parallel", "arbitrary")),
        name="branch_merge",
    )(att, rw, w_up_att, w_up_rwkv, gates, gates)


def _cumsum_kernel(f_ref, b_ref, o_ref, *, n_chunks):
    h = f_ref.shape[0]
    row = lax.broadcasted_iota(jnp.int32, (LANES, LANES), 0)
    col = lax.broadcasted_iota(jnp.int32, (LANES, LANES), 1)
    upper = (row <= col).astype(BF16)
    bias = b_ref[...]

    def body(c, carry):
        off = pl.multiple_of(c * LANES, LANES)
        z = f_ref[:, pl.ds(off, LANES)] + bias
        log_f = -_softplus(-z)
        hi = log_f.astype(BF16)
        r1 = log_f - hi.astype(F32)
        mid = r1.astype(BF16)
        lo = (r1 - mid.astype(F32)).astype(BF16)
        cs = (jnp.dot(hi, upper, preferred_element_type=F32)
              + jnp.dot(mid, upper, preferred_element_type=F32)
              + jnp.dot(lo, upper, preferred_element_type=F32)) + carry
        o_ref[:, pl.ds(off, LANES)] = cs
        return cs[:, LANES - 1:LANES]

    lax.fori_loop(0, n_chunks, body, jnp.zeros((h, 1), F32))


def _forget_cumsum(f_t, bias_col):
    h, s = f_t.shape
    assert s % LANES == 0
    return pl.pallas_call(
        functools.partial(_cumsum_kernel, n_chunks=s // LANES),
        grid=(1,),
        in_specs=[pl.BlockSpec((h, s), lambda i: (0, 0)), pl.BlockSpec((h, 1), lambda i: (0, 0))],
        out_specs=pl.BlockSpec((h, s), lambda i: (0, 0)),
        out_shape=jax.ShapeDtypeStruct((h, s), F32),
        compiler_params=_params(("arbitrary",)),
        name="forget_cumsum",
    )(f_t, bias_col)


LOG2E = 1.4426950408889634
FOX_BLOCK = 1024


def _fox_kernel(q_ref, k_ref, v_ref, ck_ref, cq_ref, o_ref, m_sc, l_sc, acc_sc, *, tq):
    i = pl.program_id(1)
    nsub = tq // LANES
    m_sc[...] = jnp.full(m_sc.shape, NEG_INF, F32)
    l_sc[...] = jnp.zeros(l_sc.shape, F32)
    acc_sc[...] = jnp.zeros(acc_sc.shape, F32)
    q = q_ref[...]
    cq = jnp.broadcast_to(cq_ref[...] * LOG2E, (tq, LANES))

    def step(j, masked):
        off = pl.multiple_of(j * tq, tq)
        k = k_ref[pl.ds(off, tq), :]
        v = v_ref[pl.ds(off, tq), :]
        s = lax.dot_general(q, k, (((1,), (1,)), ((), ())), preferred_element_type=F32)
        ck = ck_ref[:, pl.ds(off, tq)] * LOG2E
        ts = []
        for c in range(nsub):
            t = s[:, c * LANES:(c + 1) * LANES] - ck[:, c * LANES:(c + 1) * LANES]
            if masked:
                r = lax.broadcasted_iota(jnp.int32, (tq, LANES), 0)
                cc = lax.broadcasted_iota(jnp.int32, (tq, LANES), 1) + c * LANES
                t = jnp.where(cc <= r, t, NEG_INF)
            ts.append(t)
        mloc = ts[0]
        for c in range(1, nsub):
            mloc = jnp.maximum(mloc, ts[c])
        m_prev = m_sc[...]
        m_new = jnp.maximum(m_prev, jnp.max(mloc, axis=1, keepdims=True) + cq)
        alpha = jnp.exp2(m_prev - m_new)
        shift = m_new - cq
        ps = [jnp.exp2(t - shift) for t in ts]
        lsum = ps[0]
        for c in range(1, nsub):
            lsum = lsum + ps[c]
        p = jnp.concatenate([x.astype(v.dtype) for x in ps], axis=1)
        l_sc[...] = alpha * l_sc[...] + lsum
        acc_sc[...] = alpha * acc_sc[...] + jnp.dot(p, v, preferred_element_type=F32)
        m_sc[...] = m_new

    def body(j, carry):
        step(j, False)
        return carry

    lax.fori_loop(0, i, body, 0)
    step(i, True)
    l_tot = jnp.sum(l_sc[...], axis=1, keepdims=True)
    o_ref[...] = (acc_sc[...] / l_tot).astype(o_ref.dtype)


def _fox_attention(qkv, cum_k, cum_q, n_heads, dh):
    s = qkv.shape[0]
    assert dh == LANES
    tq = _pick(s, FOX_BLOCK, LANES)
    nq = s // tq
    return pl.pallas_call(
        functools.partial(_fox_kernel, tq=tq),
        grid=(n_heads, nq),
        in_specs=[pl.BlockSpec((tq, dh), lambda h, i: (i, h)),
                  pl.BlockSpec((s, dh), lambda h, i: (0, n_heads + h)),
                  pl.BlockSpec((s, dh), lambda h, i: (0, 2 * n_heads + h)),
                  pl.BlockSpec((None, 1, s), lambda h, i: (h, 0, 0)),
                  pl.BlockSpec((None, tq, 1), lambda h, i: (h, i, 0))],
        out_specs=pl.BlockSpec((tq, dh), lambda h, i: (i, h)),
        out_shape=jax.ShapeDtypeStruct((s, n_heads * dh), BF16),
        scratch_shapes=[pltpu.VMEM((tq, LANES), F32), pltpu.VMEM((tq, LANES), F32), pltpu.VMEM((tq, dh), F32)],
        compiler_params=_params(("parallel", "arbitrary")),
        name="fox_attention",
    )(qkv, qkv, qkv, cum_k, cum_q)


def _shifted(p_ref, pp_ref, mu_ref, first):
    p = p_ref[...]
    rows = p.shape[0]
    prev_last = jnp.where(first, 0.0, pp_ref[7:8, :])
    rolled = pltpu.roll(p, 1, 0)
    r = lax.broadcasted_iota(jnp.int32, (rows, 1), 0)
    p_prev = jnp.where(r == 0, prev_last, rolled)
    return p + (p_prev - p) * mu_ref[...]


INV_LEAF = 8


def _unit_lower_inverse(lm, sr, sc, n, eye):
    same = lambda b: (sr // b) == (sc // b)
    leaf = min(INV_LEAF, n)
    l0 = jnp.where(same(leaf), lm, 0.0)
    inv = eye - l0
    pw = l0
    span = 2
    while span < leaf:
        pw = _bmm(pw, pw)
        inv = inv + _bmm(inv, pw)
        span *= 2
    b = leaf
    while b < n:
        off = jnp.where(same(2 * b) & jnp.logical_not(same(b)), lm, 0.0)
        inv = inv - _bmm(inv, _bmm(off, inv))
        b *= 2
    return inv


def _bmm(a, b):
    return lax.dot_general(a.astype(BF16), b.astype(BF16), (((2,), (1,)), ((0,), (0,))),
                           preferred_element_type=F32)


def _bmm_nt(a, b):
    return lax.dot_general(a.astype(BF16), b.astype(BF16), (((2,), (2,)), ((0,), (0,))),
                           preferred_element_type=F32)


def _rwkv_kernel(pr_ref, pk_ref, pv_ref, pl_ref, ppr_ref, ppk_ref, ppv_ref, ppl_ref,
                 mur_ref, muk_ref, muv_ref, mul_ref,
                 w0_ref, w2_ref, a0_ref, a2_ref, g2_ref, kk_ref, ka_ref, rk_ref, gng_ref, gnb_ref,
                 o_ref, state_sc, *, chunk, n_chunks, head_dim, dr, ir):
    i = pl.program_id(1)
    first = i == 0

    @pl.when(first)
    def _():
        state_sc[...] = jnp.zeros(state_sc.shape, F32)

    r = _shifted(pr_ref, ppr_ref, mur_ref, first)
    k = _shifted(pk_ref, ppk_ref, muk_ref, first)
    v = _shifted(pv_ref, ppv_ref, muv_ref, first)
    low = _shifted(pl_ref, ppl_ref, mul_ref, first)
    wd = low[:, :dr]
    ad = low[:, dr:dr + ir]
    gd = low[:, dr + ir:]

    z = w0_ref[...] + _bdot(jnp.tanh(wd), w2_ref[...])
    log_w = -jnp.exp(-_softplus(-z) - 0.5)
    a = jax.nn.sigmoid(a0_ref[...] + _bdot(ad, a2_ref[...]))
    g = _bdot(jax.nn.sigmoid(gd), g2_ref[...])

    lane_r = lax.broadcasted_iota(jnp.int32, (LANES, LANES), 0)
    lane_c = lax.broadcasted_iota(jnp.int32, (LANES, LANES), 1)
    same_head = (lane_r // head_dim) == (lane_c // head_dim)
    ones_bd = same_head.astype(BF16)

    kk_raw = k * kk_ref[...]
    ss = _dot_hl(kk_raw * kk_raw, ones_bd)
    kappa = kk_raw / jnp.maximum(jnp.sqrt(ss), 1e-12)
    k_mod = k * (1.0 + (a - 1.0) * ka_ref[...])
    beta = kappa * a

    c2 = 2 * chunk
    lane = lax.broadcasted_iota(jnp.int32, (1, LANES), 1)
    head0 = lane < head_dim
    tr = lax.broadcasted_iota(jnp.int32, (chunk, chunk), 0)
    tc = lax.broadcasted_iota(jnp.int32, (chunk, chunk), 1)
    tri = (tc <= tr).astype(BF16)
    sr = lax.broadcasted_iota(jnp.int32, (c2, c2), 0)
    sc = lax.broadcasted_iota(jnp.int32, (c2, c2), 1)
    same_blk = (sr // chunk) == (sc // chunk)
    strict = same_blk & ((sc % chunk) < (sr % chunk))
    incl = same_blk & ((sc % chunk) <= (sr % chunk))
    eye = (sr == sc).astype(F32)

    ng = n_chunks
    tm = ng * chunk

    def chunked(x):
        return x.reshape(ng, chunk, LANES)

    def stack(x):
        return jnp.concatenate([jnp.where(head0, x, 0.0), jnp.where(head0, 0.0, x)], axis=1)

    lw = chunked(log_w)
    tri_b = jnp.broadcast_to(tri, (ng, chunk, chunk))
    hi = lw.astype(BF16)
    r1 = lw - hi.astype(F32)
    mid = r1.astype(BF16)
    lo = (r1 - mid.astype(F32)).astype(BF16)
    cum = _bmm(tri_b, hi) + _bmm(tri_b, mid) + _bmm(tri_b, lo)
    e_inc = jnp.exp(cum)
    e_neg = jnp.exp(-cum)
    e_exc = jnp.exp(cum - lw)
    rs = stack(chunked(r) * e_inc)
    ks = stack(chunked(k_mod) * e_neg)
    bs = stack(chunked(beta) * e_neg)
    kps = stack(chunked(kappa) * e_exc)
    vs = stack(chunked(v))
    a_kk = jnp.where(strict, _bmm_nt(kps, ks), 0.0)
    lm = jnp.where(strict, _bmm_nt(kps, bs), 0.0)
    a_rk = jnp.where(incl, _bmm_nt(rs, ks), 0.0)
    a_rb = jnp.where(incl, _bmm_nt(rs, bs), 0.0)
    tinv = _unit_lower_inverse(lm, sr, sc, chunk, eye)
    wk = _bmm(tinv, kps)
    u0 = _bmm(tinv, _bmm(a_kk, vs))
    qe = rs - _bmm(a_rb, wk)
    y0 = _bmm(a_rk, vs) - _bmm(a_rb, u0)
    mp = [_bdot_tn(wk[c], bs[c]) for c in range(ng)]
    g0 = [_bdot_tn(vs[c], ks[c]) - _bdot_tn(u0[c], bs[c]) for c in range(ng)]

    ys = []
    state = state_sc[...]
    for c in range(ng):
        y_st = _bdot_nt(qe[c], state) + y0[c]
        ys.append(y_st[:chunk] + y_st[chunk:])
        state = (state - _bdot(state, mp[c]) + g0[c]) * e_inc[c, chunk - 1:chunk, :]
    state_sc[...] = state
    y = jnp.concatenate(ys, axis=0) if ng > 1 else ys[0]
    assert y.shape == (tm, LANES)

    avg_bd = ones_bd * (1.0 / head_dim)
    mean = _dot_hl(y, avg_bd)
    dlt = y - mean
    var = _dot_hl(dlt * dlt, avg_bd)
    yn = dlt * lax.rsqrt(var + RWKV_GN_EPS) * gng_ref[...] + gnb_ref[...]
    bonus = _dot_hl(r * k_mod * rk_ref[...], ones_bd) * v
    o_ref[...] = ((yn + bonus) * g).astype(o_ref.dtype)


def _rwkv_mix(p, mu, w0, w2, a0, a2, g2, k_k, k_a, r_k, gn_g, gn_b, head_dim):
    s, rc = p.shape
    cw = w0.shape[1]
    dr, ir, gr = w2.shape[0], a2.shape[0], g2.shape[0]
    lw = dr + ir + gr
    assert head_dim * 2 == LANES and cw % LANES == 0 and (head_dim & (head_dim - 1)) == 0
    assert (3 * cw) % lw == 0 and rc == 3 * cw + lw
    chunk = min(RWKV_CHUNK, s)
    tm = _pick(s, RWKV_TILE_CHUNKS * chunk, chunk)
    n_chunks = tm // chunk
    npair = cw // LANES
    low_blk = (3 * cw) // lw
    t8 = tm // 8

    def prev(col_fn):
        return lambda h, i: (jnp.maximum(i * t8 - 1, 0), col_fn(h))

    main = lambda off: pl.BlockSpec((tm, LANES), lambda h, i: (i, off + h))
    prv = lambda off: pl.BlockSpec((8, LANES), prev(lambda h: off + h))
    vec = lambda off: pl.BlockSpec((1, LANES), lambda h, i: (0, off + h))
    in_specs = [
        main(0), main(npair), main(2 * npair), pl.BlockSpec((tm, lw), lambda h, i: (i, low_blk)),
        prv(0), prv(npair), prv(2 * npair), pl.BlockSpec((8, lw), prev(lambda h: low_blk)),
        vec(0), vec(npair), vec(2 * npair), pl.BlockSpec((1, lw), lambda h, i: (0, low_blk)),
        vec(0), pl.BlockSpec((dr, LANES), lambda h, i: (0, h)),
        vec(0), pl.BlockSpec((ir, LANES), lambda h, i: (0, h)),
        pl.BlockSpec((gr, LANES), lambda h, i: (0, h)),
        vec(0), vec(0), vec(0), vec(0), vec(0),
    ]
    return pl.pallas_call(
        functools.partial(_rwkv_kernel, chunk=chunk, n_chunks=n_chunks, head_dim=head_dim, dr=dr, ir=ir),
        grid=(npair, s // tm),
        in_specs=in_specs,
        out_specs=pl.BlockSpec((tm, LANES), lambda h, i: (i, h)),
        out_shape=jax.ShapeDtypeStruct((s, cw), BF16),
        scratch_shapes=[pltpu.VMEM((LANES, LANES), F32)],
        compiler_params=_params(("parallel", "arbitrary")),
        name="rwkv7_mix",
    )(p, p, p, p, p, p, p, p, mu, mu, mu, mu, w0, w2, a0, a2, g2, k_k, k_a, r_k, gn_g, gn_b)


def _layer_norm_rows(z, g, b):
    mu = jnp.mean(z, axis=-1, keepdims=True)
    zc = z - mu
    var = jnp.mean(zc * zc, axis=-1, keepdims=True)
    return zc * lax.rsqrt(var + LN_EPS) * g + b


def _ln1_kernel(x_ref, mix_ref, gate_ref, g_ref, b_ref, sc_ref, sh_ref, rwh_ref, rwl_ref, rb_ref,
                x1_ref, h_ref, lg_ref, *, alpha):
    z = alpha * x_ref[...] + gate_ref[...] * mix_ref[...].astype(F32)
    x1 = _layer_norm_rows(z, g_ref[...], b_ref[...])
    x1_ref[...] = x1
    h = x1 * (1.0 + sc_ref[...]) + sh_ref[...]
    tm, d = h.shape
    nc = d // LANES
    for c in range(nc):
        h_ref[pl.ds(c, tm, stride=nc), :] = h[:, c * LANES:(c + 1) * LANES]
    hi, lo = _split2(h)
    wh = rwh_ref[...]
    wl = rwl_ref[...]
    lg_ref[...] = (jnp.dot(hi, wh, preferred_element_type=F32) + jnp.dot(lo, wh, preferred_element_type=F32)
                   + jnp.dot(hi, wl, preferred_element_type=F32)) + rb_ref[...]


def _ln1_router(x, mix, gate, g, b, scale, shift, rw_hi, rw_lo, rb, alpha):
    s, d = x.shape
    ne = rw_hi.shape[1]
    tm = _pick(s, 256)
    row = pl.BlockSpec((1, d), lambda i: (0, 0))
    tile = pl.BlockSpec((tm, d), lambda i: (i, 0))
    wspec = pl.BlockSpec((d, ne), lambda i: (0, 0))
    return pl.pallas_call(
        functools.partial(_ln1_kernel, alpha=alpha),
        grid=(s // tm,),
        in_specs=[tile, tile, row, row, row, row, row, wspec, wspec, pl.BlockSpec((1, ne), lambda i: (0, 0))],
        out_specs=[tile, pl.BlockSpec((tm * (d // LANES), LANES), lambda i: (i, 0)),
                   pl.BlockSpec((tm, ne), lambda i: (i, 0))],
        out_shape=[jax.ShapeDtypeStruct((s, d), F32), jax.ShapeDtypeStruct((s * (d // LANES), LANES), F32),
                   jax.ShapeDtypeStruct((s, ne), F32)],
        compiler_params=_params(("parallel",)),
        name="ln1_router",
    )(x, mix, gate, g, b, scale, shift, rw_hi, rw_lo, rb)


def _slab_pitch(nc):
    p = -(-nc // SUBLANES)
    return SUBLANES * (p if p % 2 else p + 1)


def _gather_kernel(tok_ref, na_ref, h_hbm, o_ref, buf, sem, *, tm, nc, pitch, unroll):
    b = pl.program_id(0)
    base = b * tm

    def slab_copy(r, tok):
        src = h_hbm.at[pl.ds(pl.multiple_of(tok * nc, SUBLANES), nc)]
        dst = buf.at[pl.ds(pl.multiple_of(r * pitch, SUBLANES), nc)]
        return pltpu.make_async_copy(src, dst, sem)

    def start(r0, carry):
        for u in range(unroll):
            r = r0 * unroll + u
            slab_copy(r, tok_ref[base + r]).start()
        return carry

    def wait(r0, carry):
        for u in range(unroll):
            slab_copy(r0 * unroll + u, 0).wait()
        return carry

    @pl.when(b < na_ref[0])
    def _():
        lax.fori_loop(0, tm // unroll, start, 0)
        lax.fori_loop(0, tm // unroll, wait, 0)
        for c in range(nc):
            o_ref[:, c * LANES:(c + 1) * LANES] = buf[pl.ds(c, tm, stride=pitch), :].astype(o_ref.dtype)

    @pl.when(b >= na_ref[0])
    def _():
        o_ref[...] = jnp.zeros(o_ref.shape, o_ref.dtype)


def _gather_rows(h_slab, row_tok, n_active, tm, d):
    p = row_tok.shape[0]
    nc = d // LANES
    assert nc % SUBLANES == 0 and h_slab.shape[1] == LANES
    pitch = _slab_pitch(nc)
    return pl.pallas_call(
        functools.partial(_gather_kernel, tm=tm, nc=nc, pitch=pitch, unroll=8),
        grid_spec=pltpu.PrefetchScalarGridSpec(
            num_scalar_prefetch=2,
            grid=(p // tm,),
            in_specs=[pl.BlockSpec(memory_space=pl.ANY)],
            out_specs=pl.BlockSpec((tm, d), lambda b, tok, na: (b, 0)),
            scratch_shapes=[pltpu.VMEM((tm * pitch, LANES), h_slab.dtype), pltpu.SemaphoreType.DMA(())]),
        out_shape=jax.ShapeDtypeStruct((p, d), BF16),
        compiler_params=_params(("arbitrary",)),
        name="moe_gather",
    )(row_tok, n_active, h_slab)


def _expert_kernel(be_ref, na_ref, x_ref, wg_ref, wu_ref, bg_ref, bu_ref, wd_ref, bd_ref, o_ref, *, nf):
    b = pl.program_id(0)
    f = pl.program_id(1)
    active = b < na_ref[0]

    @pl.when(f == 0)
    def _():
        o_ref[...] = jnp.zeros(o_ref.shape, F32)

    @pl.when(active)
    def _():
        x = x_ref[...]
        gate = jnp.dot(x, wg_ref[...], preferred_element_type=F32) + bg_ref[...]
        up = jnp.dot(x, wu_ref[...], preferred_element_type=F32) + bu_ref[...]
        gate = jnp.minimum(gate, SWIGLU_LIMIT)
        up = jnp.clip(up, -SWIGLU_LIMIT, SWIGLU_LIMIT)
        act = (up + 1.0) * (gate * jax.nn.sigmoid(gate * SWIGLU_ALPHA))
        o_ref[...] += jnp.dot(act.astype(BF16), wd_ref[...], preferred_element_type=F32)

    @pl.when(active & (f == nf - 1))
    def _():
        o_ref[...] += bd_ref[...]


def _expert_ffn(xg, block_e, n_active, w_gu, b_gu, w_down, b_down, tm):
    p, d = xg.shape
    ne, fdim = w_down.shape[0], w_down.shape[1]
    tf = _pick(fdim, 256, LANES)
    nf = fdim // tf

    def fidx(b, f, na):
        return jnp.where(b < na[0], f, nf - 1)

    in_specs = [
        pl.BlockSpec((tm, d), lambda b, f, be, na: (b, 0)),
        pl.BlockSpec((None, d, tf), lambda b, f, be, na: (be[b], 0, fidx(b, f, na))),
        pl.BlockSpec((None, d, tf), lambda b, f, be, na: (be[b], 0, nf + fidx(b, f, na))),
        pl.BlockSpec((None, 1, tf), lambda b, f, be, na: (be[b], 0, fidx(b, f, na))),
        pl.BlockSpec((None, 1, tf), lambda b, f, be, na: (be[b], 0, nf + fidx(b, f, na))),
        pl.BlockSpec((None, tf, d), lambda b, f, be, na: (be[b], fidx(b, f, na), 0)),
        pl.BlockSpec((None, 1, d), lambda b, f, be, na: (be[b], 0, 0)),
    ]
    return pl.pallas_call(
        functools.partial(_expert_kernel, nf=nf),
        grid_spec=pltpu.PrefetchScalarGridSpec(
            num_scalar_prefetch=2,
            grid=(p // tm, nf),
            in_specs=in_specs,
            out_specs=pl.BlockSpec((tm, d), lambda b, f, be, na: (b, 0))),
        out_shape=jax.ShapeDtypeStruct((p, d), F32),
        compiler_params=_params(("arbitrary", "arbitrary")),
        name="moe_experts",
    )(block_e, n_active, xg, w_gu, w_gu, b_gu, b_gu, w_down, b_down)


def _combine_kernel(pos_ref, y_hbm, x1_ref, gw_ref, gate_ref, g_ref, b_ref, o_ref, buf, sem, *, tm, alpha):
    base = pl.program_id(0) * tm * TOP_K

    def row_copy(r, kk, src):
        return pltpu.make_async_copy(y_hbm.at[pl.ds(src, 1)], buf.at[kk, pl.ds(r, 1)], sem)

    def start(r, carry):
        for kk in range(TOP_K):
            row_copy(r, kk, pos_ref[base + r * TOP_K + kk]).start()
        return carry

    def wait(r, carry):
        for kk in range(TOP_K):
            row_copy(r, kk, 0).wait()
        return carry

    lax.fori_loop(0, tm, start, 0)
    lax.fori_loop(0, tm, wait, 0)
    gw = gw_ref[...]
    ff = buf[0] * gw[:, 0:1]
    for kk in range(1, TOP_K):
        ff = ff + buf[kk] * gw[:, kk:kk + 1]
    z = alpha * x1_ref[...] + gate_ref[...] * ff
    o_ref[...] = _layer_norm_rows(z, g_ref[...], b_ref[...])


def _combine_ln2(y, pos, x1, gate_w, gate, g, b, alpha):
    s, d = x1.shape
    tm = _pick(s, 128)
    row = pl.BlockSpec((1, d), lambda i, pos: (0, 0))
    tile = pl.BlockSpec((tm, d), lambda i, pos: (i, 0))
    return pl.pallas_call(
        functools.partial(_combine_kernel, tm=tm, alpha=alpha),
        grid_spec=pltpu.PrefetchScalarGridSpec(
            num_scalar_prefetch=1,
            grid=(s // tm,),
            in_specs=[pl.BlockSpec(memory_space=pl.ANY), tile,
                      pl.BlockSpec((tm, TOP_K), lambda i, pos: (i, 0)), row, row, row],
            out_specs=tile,
            scratch_shapes=[pltpu.VMEM((TOP_K, tm, d), F32), pltpu.SemaphoreType.DMA(())]),
        out_shape=jax.ShapeDtypeStruct((s, d), F32),
        compiler_params=_params(("arbitrary",)),
        name="moe_combine_ln2",
    )(pos, y, x1, gate_w, gate, g, b)


def _routing_plan(logits, n_experts, tm):
    t = logits.shape[0]
    top_vals, top_idx = lax.top_k(logits, TOP_K)
    gate_w = jax.nn.softmax(top_vals, axis=-1)
    tk = t * TOP_K
    flat_e = top_idx.reshape(tk).astype(jnp.int32)
    order = jnp.argsort(flat_e).astype(jnp.int32)
    sorted_e = flat_e[order]
    counts = jnp.bincount(flat_e, length=n_experts).astype(jnp.int32)
    padded = ((counts + tm - 1) // tm) * tm
    start = jnp.cumsum(counts) - counts
    pad_end = jnp.cumsum(padded)
    pad_start = pad_end - padded
    dest = (pad_start[sorted_e] + (jnp.arange(tk, dtype=jnp.int32) - start[sorted_e])).astype(jnp.int32)
    n_blocks = -(-tk // tm) + n_experts
    row_tok = jnp.zeros((n_blocks * tm,), jnp.int32).at[dest].set(order // TOP_K)
    pos = jnp.zeros((tk,), jnp.int32).at[order].set(dest)
    block_e = jnp.minimum(jnp.searchsorted(pad_end, jnp.arange(n_blocks, dtype=jnp.int32) * tm, side="right"),
                          n_experts - 1).astype(jnp.int32)
    n_active = (pad_end[-1:] // tm).astype(jnp.int32)
    return gate_w, row_tok, pos, block_e, n_active


def _layer(x, c_col, prm):
    s, d = x.shape
    n_att_heads = prm["fox_f_bias"].shape[0]
    att_w = prm["w_up_att"].shape[0]
    dh = att_w // n_att_heads
    n_rw_heads, rw_hd = prm["rwkv_r_k"].shape
    rc = prm["rwkv_mu"].shape[0]
    n_experts = prm["router_w"].shape[1]
    depth_alpha = prm["alpha"]

    ada = _ada(c_col, prm["w_ada"], prm["b_ada"][None, :])
    mods = [ada[:, m * d:(m + 1) * d] for m in range(6)]
    shift_m, scale_m, gate_m, shift_f, scale_f, gate_f = mods

    h = _modulate(x, scale_m, shift_m)
    w_in = prm["w_in"]
    off_f = 3 * att_w
    off_rw = off_f + n_att_heads
    off_gate = off_rw + rc
    w_qkv = w_in[:, :off_f].astype(BF16)
    w_f = jnp.pad(w_in[:, off_f:off_rw], ((0, 0), (0, LANES - n_att_heads))).astype(BF16)
    w_rw = w_in[:, off_rw:off_gate].astype(BF16)
    w_gt = w_in[:, off_gate:].astype(BF16)
    q_scale = jnp.concatenate([jnp.full((1, att_w), dh ** -0.5 * LOG2E, F32), jnp.ones((1, 2 * att_w), F32)],
                              axis=1)
    qkv = _matmul(h, w_qkv, BF16, col_scale=q_scale, name="proj_qkv")
    f_logit = _matmul(h, w_f, F32, name="proj_forget")
    p_rw = _matmul(h, w_rw, F32, name="proj_rwkv")
    gates = _matmul(h, w_gt, BF16, act="sigmoid", name="proj_gates")

    cum = _forget_cumsum(f_logit[:, :n_att_heads].T, prm["fox_f_bias"][:, None])
    att = _fox_attention(qkv, cum[:, None, :], cum[:, :, None], n_att_heads, dh)

    row = lambda v: v.reshape(1, -1)
    rw = _rwkv_mix(p_rw, row(prm["rwkv_mu"]), row(prm["rwkv_w0"]), prm["rwkv_w2"].astype(BF16),
                   row(prm["rwkv_a0"]), prm["rwkv_a2"].astype(BF16), prm["rwkv_g2"].astype(BF16),
                   row(prm["rwkv_k_k"]), row(prm["rwkv_k_a"]), row(prm["rwkv_r_k"]),
                   row(prm["rwkv_gn_g"]), row(prm["rwkv_gn_b"]), rw_hd)

    merged = _merge(att, rw, prm["w_up_att"].astype(BF16), prm["w_up_rwkv"].astype(BF16), gates, d)
    mix = _matmul(merged, prm["w_o"].astype(BF16), F32, name="proj_out")

    ne_pad = -(-n_experts // LANES) * LANES
    rw_full = jnp.pad(prm["router_w"], ((0, 0), (0, ne_pad - n_experts)))
    rw_hi = rw_full.astype(BF16)
    rw_lo = (rw_full - rw_hi.astype(F32)).astype(BF16)
    rb = jnp.pad(prm["router_b"], (0, ne_pad - n_experts))[None, :]
    x1, h2, logits = _ln1_router(x, mix, gate_m, row(prm["ln1_g"]), row(prm["ln1_b"]), scale_f, shift_f,
                                 rw_hi, rw_lo, rb, depth_alpha)

    tm_e = _pick(s, 512)
    gate_w, row_tok, pos, block_e, n_active = _routing_plan(logits[:, :n_experts], n_experts, tm_e)
    xg = _gather_rows(h2, row_tok, n_active, tm_e, d)
    y = _expert_ffn(xg, block_e, n_active, prm["w_gate_up"].astype(BF16), prm["b_gate_up"][:, None, :],
                    prm["w_down"].astype(BF16), prm["b_down"][:, None, :], tm_e)
    return _combine_ln2(y, pos, x1, gate_w, gate_f, row(prm["ln2_g"]), row(prm["ln2_b"]), depth_alpha)


def kernel(x, c, w_ada, b_ada, w_in, fox_f_bias, rwkv_mu, rwkv_w0, rwkv_w2, rwkv_a0, rwkv_a2, rwkv_g2,
           rwkv_k_k, rwkv_k_a, rwkv_r_k, rwkv_gn_g, rwkv_gn_b, w_up_att, w_up_rwkv, w_o, ln1_g, ln1_b,
           router_w, router_b, w_gate_up, b_gate_up, w_down, b_down, ln2_g, ln2_b):
    stacked = dict(w_ada=w_ada, b_ada=b_ada, w_in=w_in, fox_f_bias=fox_f_bias, rwkv_mu=rwkv_mu,
                   rwkv_w0=rwkv_w0, rwkv_w2=rwkv_w2, rwkv_a0=rwkv_a0, rwkv_a2=rwkv_a2, rwkv_g2=rwkv_g2,
                   rwkv_k_k=rwkv_k_k, rwkv_k_a=rwkv_k_a, rwkv_r_k=rwkv_r_k, rwkv_gn_g=rwkv_gn_g,
                   rwkv_gn_b=rwkv_gn_b, w_up_att=w_up_att, w_up_rwkv=w_up_rwkv, w_o=w_o, ln1_g=ln1_g,
                   ln1_b=ln1_b, router_w=router_w, router_b=router_b, w_gate_up=w_gate_up,
                   b_gate_up=b_gate_up, w_down=w_down, b_down=b_down, ln2_g=ln2_g, ln2_b=ln2_b)
    depth = w_ada.shape[0]
    alpha = (2 * depth) ** 0.25
    outs = []
    for bi in range(x.shape[0]):
        xb = x[bi]
        c_col = c[bi][:, None]
        for layer in range(depth):
            prm = {name: val[layer] for name, val in stacked.items()}
            prm["alpha"] = alpha
            xb = _layer(xb, c_col, prm)
        outs.append(xb)
    return jnp.stack(outs, axis=0)
```

```python
import functools

import jax
import jax.numpy as jnp
from jax import lax
from jax.experimental import pallas as pl
from jax.experimental.pallas import tpu as pltpu

F32 = jnp.float32
BF16 = jnp.bfloat16

TOP_K = 4
SWIGLU_LIMIT = 7.0
SWIGLU_ALPHA = 1.702
LN_EPS = 1e-5
RWKV_GN_EPS = 64e-5

LANES = 128
SUBLANES = 8
VMEM_LIMIT_BYTES = 56 * 1024 * 1024
RWKV_CHUNK = 64
RWKV_TILE_CHUNKS = 16
NEG_INF = float("-inf")


def _pick(n, pref, align=8):
    if n <= pref:
        return n
    t = (pref // align) * align
    while t >= align:
        if n % t == 0:
            return t
        t -= align
    return n


def _params(sem):
    return pltpu.CompilerParams(dimension_semantics=sem, vmem_limit_bytes=VMEM_LIMIT_BYTES)


def _bdot(a, b):
    return jnp.dot(a.astype(BF16), b.astype(BF16), preferred_element_type=F32)


def _bdot_nt(a, b):
    return lax.dot_general(a.astype(BF16), b.astype(BF16), (((1,), (1,)), ((), ())),
                           preferred_element_type=F32)


def _bdot_tn(a, b):
    return lax.dot_general(a.astype(BF16), b.astype(BF16), (((0,), (0,)), ((), ())),
                           preferred_element_type=F32)


def _split2(x):
    hi = x.astype(BF16)
    lo = (x - hi.astype(F32)).astype(BF16)
    return hi, lo


def _dot_hl(x, m):
    hi, lo = _split2(x)
    return jnp.dot(hi, m, preferred_element_type=F32) + jnp.dot(lo, m, preferred_element_type=F32)


def _dot_lh3(m, x):
    hi = x.astype(BF16)
    r1 = x - hi.astype(F32)
    mid = r1.astype(BF16)
    lo = (r1 - mid.astype(F32)).astype(BF16)
    return (jnp.dot(m, hi, preferred_element_type=F32) + jnp.dot(m, mid, preferred_element_type=F32)
            + jnp.dot(m, lo, preferred_element_type=F32))


def _softplus(u):
    return jnp.maximum(u, 0.0) + jnp.log(1.0 + jnp.exp(-jnp.abs(u)))


def _ada_kernel(c_ref, w_ref, b_ref, o_ref):
    cv = c_ref[...]
    cond = cv * jax.nn.sigmoid(cv)
    o_ref[...] = jnp.sum(w_ref[...] * cond, axis=0, keepdims=True) + b_ref[...]


def _ada(c_col, w, b_row):
    d, n = w.shape
    tn = _pick(n, 512, LANES)
    return pl.pallas_call(
        _ada_kernel,
        grid=(n // tn,),
        in_specs=[pl.BlockSpec((d, 1), lambda j: (0, 0)),
                  pl.BlockSpec((d, tn), lambda j: (0, j)),
                  pl.BlockSpec((1, tn), lambda j: (0, j))],
        out_specs=pl.BlockSpec((1, tn), lambda j: (0, j)),
        out_shape=jax.ShapeDtypeStruct((1, n), F32),
        compiler_params=_params(("parallel",)),
        name="ada_matvec",
    )(c_col, w, b_row)


def _modulate_kernel(x_ref, sc_ref, sh_ref, o_ref):
    o_ref[...] = (x_ref[...] * (1.0 + sc_ref[...]) + sh_ref[...]).astype(o_ref.dtype)


def _modulate(x, scale, shift):
    s, d = x.shape
    tm = _pick(s, 512)
    row = pl.BlockSpec((1, d), lambda i: (0, 0))
    return pl.pallas_call(
        _modulate_kernel,
        grid=(s // tm,),
        in_specs=[pl.BlockSpec((tm, d), lambda i: (i, 0)), row, row],
        out_specs=pl.BlockSpec((tm, d), lambda i: (i, 0)),
        out_shape=jax.ShapeDtypeStruct((s, d), BF16),
        compiler_params=_params(("parallel",)),
        name="modulate",
    )(x, scale, shift)


def _mm_kernel(a_ref, w_ref, *rest, has_scale, act):
    o_ref = rest[-1]
    acc = jnp.dot(a_ref[...], w_ref[...], preferred_element_type=F32)
    if has_scale:
        acc = acc * rest[0][...]
    if act == "sigmoid":
        acc = jax.nn.sigmoid(acc)
    o_ref[...] = acc.astype(o_ref.dtype)


def _matmul(a, w, out_dtype, col_scale=None, act=None, tm_pref=1024, tn_pref=512, name="matmul"):
    m, k = a.shape
    n = w.shape[1]
    tm = _pick(m, tm_pref)
    tn = _pick(n, tn_pref, LANES)
    in_specs = [pl.BlockSpec((tm, k), lambda i, j: (i, 0)),
                pl.BlockSpec((k, tn), lambda i, j: (0, j))]
    args = [a, w]
    if col_scale is not None:
        in_specs.append(pl.BlockSpec((1, tn), lambda i, j: (0, j)))
        args.append(col_scale)
    return pl.pallas_call(
        functools.partial(_mm_kernel, has_scale=col_scale is not None, act=act),
        grid=(m // tm, n // tn),
        in_specs=in_specs,
        out_specs=pl.BlockSpec((tm, tn), lambda i, j: (i, j)),
        out_shape=jax.ShapeDtypeStruct((m, n), out_dtype),
        compiler_params=_params(("parallel", "arbitrary")),
        name=name,
    )(*args)


def _merge_kernel(att_ref, rw_ref, wa_ref, wr_ref, ga_ref, gr_ref, o_ref):
    ya = jnp.dot(att_ref[...], wa_ref[...], preferred_element_type=F32)
    yr = jnp.dot(rw_ref[...], wr_ref[...], preferred_element_type=F32)
    o_ref[...] = (ga_ref[...].astype(F32) * ya + gr_ref[...].astype(F32) * yr).astype(o_ref.dtype)


def _merge(att, rw, w_up_att, w_up_rwkv, gates, d):
    s, ka = att.shape
    kr = rw.shape[1]
    tm = _pick(s, 1024)
    tn = _pick(d, 512, LANES)
    nj = d // tn
    return pl.pallas_call(
        _merge_kernel,
        grid=(s // tm, nj),
        in_specs=[pl.BlockSpec((tm, ka), lambda i, j: (i, 0)),
                  pl.BlockSpec((tm, kr), lambda i, j: (i, 0)),
                  pl.BlockSpec((ka, tn), lambda i, j: (0, j)),
                  pl.BlockSpec((kr, tn), lambda i, j: (0, j)),
                  pl.BlockSpec((tm, tn), lambda i, j: (i, j)),
                  pl.BlockSpec((tm, tn), lambda i, j: (i, j + nj))],
        out_specs=pl.BlockSpec((tm, tn), lambda i, j: (i, j)),
        out_shape=jax.ShapeDtypeStruct((s, d), BF16),
        compiler_params=_params(("parallel", "arbitrary")),
        name="branch_merge",
    )(att, rw, w_up_att, w_up_rwkv, gates, gates)


def _cumsum_kernel(f_ref, b_ref, o_ref, *, n_chunks):
    h = f_ref.shape[0]
    row = lax.broadcasted_iota(jnp.int32, (LANES, LANES), 0)
    col = lax.broadcasted_iota(jnp.int32, (LANES, LANES), 1)
    upper = (row <= col).astype(BF16)
    bias = b_ref[...]

    def body(c, carry):
        off = pl.multiple_of(c * LANES, LANES)
        z = f_ref[:, pl.ds(off, LANES)] + bias
        log_f = -_softplus(-z)
        hi = log_f.astype(BF16)
        r1 = log_f - hi.astype(F32)
        mid = r1.astype(BF16)
        lo = (r1 - mid.astype(F32)).astype(BF16)
        cs = (jnp.dot(hi, upper, preferred_element_type=F32)
              + jnp.dot(mid, upper, preferred_element_type=F32)
              + jnp.dot(lo, upper, preferred_element_type=F32)) + carry
        o_ref[:, pl.ds(off, LANES)] = cs
        return cs[:, LANES - 1:LANES]

    lax.fori_loop(0, n_chunks, body, jnp.zeros((h, 1), F32))


def _forget_cumsum(f_t, bias_col):
    h, s = f_t.shape
    assert s % LANES == 0
    return pl.pallas_call(
        functools.partial(_cumsum_kernel, n_chunks=s // LANES),
        grid=(1,),
        in_specs=[pl.BlockSpec((h, s), lambda i: (0, 0)), pl.BlockSpec((h, 1), lambda i: (0, 0))],
        out_specs=pl.BlockSpec((h, s), lambda i: (0, 0)),
        out_shape=jax.ShapeDtypeStruct((h, s), F32),
        compiler_params=_params(("arbitrary",)),
        name="forget_cumsum",
    )(f_t, bias_col)


LOG2E = 1.4426950408889634
FOX_BLOCK = 1024


def _fox_kernel(q_ref, k_ref, v_ref, ck_ref, o_ref, m_sc, l_sc, acc_sc, *, tq):
    i = pl.program_id(1)
    nsub = tq // LANES
    m_sc[...] = jnp.full(m_sc.shape, NEG_INF, F32)
    l_sc[...] = jnp.zeros(l_sc.shape, F32)
    acc_sc[...] = jnp.zeros(acc_sc.shape, F32)
    q = q_ref[...]
    cq_parts = []
    for rblk in range(nsub):
        roff = pl.multiple_of(i * tq + rblk * LANES, LANES)
        rowv = ck_ref[:, pl.ds(roff, LANES)] * LOG2E
        cq_parts.append(jnp.transpose(jnp.broadcast_to(rowv, (LANES, LANES))))
    cq = jnp.concatenate(cq_parts, axis=0)

    def step(j, masked):
        off = pl.multiple_of(j * tq, tq)
        k = k_ref[pl.ds(off, tq), :]
        v = v_ref[pl.ds(off, tq), :]
        s = lax.dot_general(q, k, (((1,), (1,)), ((), ())), preferred_element_type=F32)
        ck = ck_ref[:, pl.ds(off, tq)] * LOG2E
        ts = []
        for c in range(nsub):
            t = s[:, c * LANES:(c + 1) * LANES] - ck[:, c * LANES:(c + 1) * LANES]
            if masked:
                r = lax.broadcasted_iota(jnp.int32, (tq, LANES), 0)
                cc = lax.broadcasted_iota(jnp.int32, (tq, LANES), 1) + c * LANES
                t = jnp.where(cc <= r, t, NEG_INF)
            ts.append(t)
        mloc = ts[0]
        for c in range(1, nsub):
            mloc = jnp.maximum(mloc, ts[c])
        m_prev = m_sc[...]
        m_new = jnp.maximum(m_prev, jnp.max(mloc, axis=1, keepdims=True) + cq)
        alpha = jnp.exp2(m_prev - m_new)
        shift = m_new - cq
        ps = [jnp.exp2(t - shift) for t in ts]
        lsum = ps[0]
        for c in range(1, nsub):
            lsum = lsum + ps[c]
        p = jnp.concatenate([x.astype(v.dtype) for x in ps], axis=1)
        l_sc[...] = alpha * l_sc[...] + lsum
        acc_sc[...] = alpha * acc_sc[...] + jnp.dot(p, v, preferred_element_type=F32)
        m_sc[...] = m_new

    def body(j, carry):
        step(j, False)
        return carry

    lax.fori_loop(0, i, body, 0)
    step(i, True)
    l_tot = jnp.sum(l_sc[...], axis=1, keepdims=True)
    o_ref[...] = (acc_sc[...] / l_tot).astype(o_ref.dtype)


def _fox_attention(qkv, cum_k, n_heads, dh):
    s = qkv.shape[0]
    assert dh == LANES
    tq = _pick(s, FOX_BLOCK, LANES)
    nq = s // tq
    return pl.pallas_call(
        functools.partial(_fox_kernel, tq=tq),
        grid=(n_heads, nq),
        in_specs=[pl.BlockSpec((tq, dh), lambda h, i: (i, h)),
                  pl.BlockSpec((s, dh), lambda h, i: (0, n_heads + h)),
                  pl.BlockSpec((s, dh), lambda h, i: (0, 2 * n_heads + h)),
                  pl.BlockSpec((None, 1, s), lambda h, i: (h, 0, 0))],
        out_specs=pl.BlockSpec((tq, dh), lambda h, i: (i, h)),
        out_shape=jax.ShapeDtypeStruct((s, n_heads * dh), BF16),
        scratch_shapes=[pltpu.VMEM((tq, LANES), F32), pltpu.VMEM((tq, LANES), F32), pltpu.VMEM((tq, dh), F32)],
        compiler_params=_params(("parallel", "arbitrary")),
        name="fox_attention",
    )(qkv, qkv, qkv, cum_k)


def _shifted(p_ref, pp_ref, mu_ref, first):
    p = p_ref[...]
    rows = p.shape[0]
    prev_last = jnp.where(first, 0.0, pp_ref[7:8, :])
    rolled = pltpu.roll(p, 1, 0)
    r = lax.broadcasted_iota(jnp.int32, (rows, 1), 0)
    p_prev = jnp.where(r == 0, prev_last, rolled)
    return p + (p_prev - p) * mu_ref[...]


INV_LEAF = 8


def _unit_lower_inverse(lm, sr, sc, n, eye):
    same = lambda b: (sr // b) == (sc // b)
    leaf = min(INV_LEAF, n)
    l0 = jnp.where(same(leaf), lm, 0.0)
    inv = eye - l0
    pw = l0
    span = 2
    while span < leaf:
        pw = _bmm(pw, pw)
        inv = inv + _bmm(inv, pw)
        span *= 2
    b = leaf
    while b < n:
        off = jnp.where(same(2 * b) & jnp.logical_not(same(b)), lm, 0.0)
        inv = inv - _bmm(inv, _bmm(off, inv))
        b *= 2
    return inv


def _bmm(a, b):
    return lax.dot_general(a.astype(BF16), b.astype(BF16), (((2,), (1,)), ((0,), (0,))),
                           preferred_element_type=F32)


def _bmm_nt(a, b):
    return lax.dot_general(a.astype(BF16), b.astype(BF16), (((2,), (2,)), ((0,), (0,))),
                           preferred_element_type=F32)


def _rwkv_kernel(pr_ref, pk_ref, pv_ref, pl_ref, ppr_ref, ppk_ref, ppv_ref, ppl_ref,
                 mur_ref, muk_ref, muv_ref, mul_ref,
                 w0_ref, w2_ref, a0_ref, a2_ref, g2_ref, kk_ref, ka_ref, rk_ref, gng_ref, gnb_ref,
                 o_ref, state_sc, *, chunk, n_chunks, head_dim, dr, ir):
    i = pl.program_id(1)
    first = i == 0

    @pl.when(first)
    def _():
        state_sc[...] = jnp.zeros(state_sc.shape, F32)

    r = _shifted(pr_ref, ppr_ref, mur_ref, first)
    k = _shifted(pk_ref, ppk_ref, muk_ref, first)
    v = _shifted(pv_ref, ppv_ref, muv_ref, first)
    low = _shifted(pl_ref, ppl_ref, mul_ref, first)
    wd = low[:, :dr]
    ad = low[:, dr:dr + ir]
    gd = low[:, dr + ir:]

    z = w0_ref[...] + _bdot(jnp.tanh(wd), w2_ref[...])
    log_w = -jnp.exp(-_softplus(-z) - 0.5)
    a = jax.nn.sigmoid(a0_ref[...] + _bdot(ad, a2_ref[...]))
    g = _bdot(jax.nn.sigmoid(gd), g2_ref[...])

    lane_r = lax.broadcasted_iota(jnp.int32, (LANES, LANES), 0)
    lane_c = lax.broadcasted_iota(jnp.int32, (LANES, LANES), 1)
    same_head = (lane_r // head_dim) == (lane_c // head_dim)
    ones_bd = same_head.astype(BF16)

    kk_raw = k * kk_ref[...]
    ss = _dot_hl(kk_raw * kk_raw, ones_bd)
    kappa = kk_raw / jnp.maximum(jnp.sqrt(ss), 1e-12)
    k_mod = k * (1.0 + (a - 1.0) * ka_ref[...])
    beta = kappa * a

    c2 = 2 * chunk
    lane = lax.broadcasted_iota(jnp.int32, (1, LANES), 1)
    head0 = lane < head_dim
    tr = lax.broadcasted_iota(jnp.int32, (chunk, chunk), 0)
    tc = lax.broadcasted_iota(jnp.int32, (chunk, chunk), 1)
    tri = (tc <= tr).astype(BF16)
    sr = lax.broadcasted_iota(jnp.int32, (c2, c2), 0)
    sc = lax.broadcasted_iota(jnp.int32, (c2, c2), 1)
    same_blk = (sr // chunk) == (sc // chunk)
    strict = same_blk & ((sc % chunk) < (sr % chunk))
    incl = same_blk & ((sc % chunk) <= (sr % chunk))
    eye = (sr == sc).astype(F32)

    ng = n_chunks
    tm = ng * chunk

    def chunked(x):
        return x.reshape(ng, chunk, LANES)

    def stack(x):
        return jnp.concatenate([jnp.where(head0, x, 0.0), jnp.where(head0, 0.0, x)], axis=1)

    lw = chunked(log_w)
    tri_b = jnp.broadcast_to(tri, (ng, chunk, chunk))
    hi = lw.astype(BF16)
    r1 = lw - hi.astype(F32)
    mid = r1.astype(BF16)
    lo = (r1 - mid.astype(F32)).astype(BF16)
    cum = _bmm(tri_b, hi) + _bmm(tri_b, mid) + _bmm(tri_b, lo)
    e_inc = jnp.exp(cum)
    e_neg = jnp.exp(-cum)
    e_exc = jnp.exp(cum - lw)
    rs = stack(chunked(r) * e_inc)
    ks = stack(chunked(k_mod) * e_neg)
    bs = stack(chunked(beta) * e_neg)
    kps = stack(chunked(kappa) * e_exc)
    vs = stack(chunked(v))
    prod = _bmm_nt(jnp.concatenate([kps, rs], axis=1), jnp.concatenate([ks, bs], axis=1))
    a_kk = jnp.where(strict, prod[:, :c2, :c2], 0.0)
    lm = jnp.where(strict, prod[:, :c2, c2:], 0.0)
    a_rk = jnp.where(incl, prod[:, c2:, :c2], 0.0)
    a_rb = jnp.where(incl, prod[:, c2:, c2:], 0.0)
    tinv = _unit_lower_inverse(lm, sr, sc, chunk, eye)
    av = _bmm(jnp.concatenate([a_kk, a_rk], axis=1), vs)
    wu = _bmm(tinv, jnp.concatenate([kps, av[:, :c2]], axis=2))
    rbw = _bmm(a_rb, wu)
    qe = rs - rbw[:, :, :LANES]
    y0 = av[:, c2:] - rbw[:, :, LANES:]
    mp, g0 = [], []
    for c in range(ng):
        tb = _bdot_tn(wu[c], bs[c])
        mp.append(tb[:LANES])
        g0.append(_bdot_tn(vs[c], ks[c]) - tb[LANES:])

    ys = []
    state = state_sc[...]
    for c in range(ng):
        y_st = _bdot_nt(qe[c], state) + y0[c]
        ys.append(y_st[:chunk] + y_st[chunk:])
        state = (state - _bdot(state, mp[c]) + g0[c]) * e_inc[c, chunk - 1:chunk, :]
    state_sc[...] = state
    y = jnp.concatenate(ys, axis=0) if ng > 1 else ys[0]
    assert y.shape == (tm, LANES)

    avg_bd = ones_bd * (1.0 / head_dim)
    mean = _dot_hl(y, avg_bd)
    dlt = y - mean
    var = _dot_hl(dlt * dlt, avg_bd)
    yn = dlt * lax.rsqrt(var + RWKV_GN_EPS) * gng_ref[...] + gnb_ref[...]
    bonus = _dot_hl(r * k_mod * rk_ref[...], ones_bd) * v
    o_ref[...] = ((yn + bonus) * g).astype(o_ref.dtype)


def _rwkv_mix(p, mu, w0, w2, a0, a2, g2, k_k, k_a, r_k, gn_g, gn_b, head_dim):
    s, rc = p.shape
    cw = w0.shape[1]
    dr, ir, gr = w2.shape[0], a2.shape[0], g2.shape[0]
    lw = dr + ir + gr
    assert head_dim * 2 == LANES and cw % LANES == 0 and (head_dim & (head_dim - 1)) == 0
    assert (3 * cw) % lw == 0 and rc == 3 * cw + lw
    chunk = min(RWKV_CHUNK, s)
    tm = _pick(s, RWKV_TILE_CHUNKS * chunk, chunk)
    n_chunks = tm // chunk
    npair = cw // LANES
    low_blk = (3 * cw) // lw
    t8 = tm // 8

    def prev(col_fn):
        return lambda h, i: (jnp.maximum(i * t8 - 1, 0), col_fn(h))

    main = lambda off: pl.BlockSpec((tm, LANES), lambda h, i: (i, off + h))
    prv = lambda off: pl.BlockSpec((8, LANES), prev(lambda h: off + h))
    vec = lambda off: pl.BlockSpec((1, LANES), lambda h, i: (0, off + h))
    in_specs = [
        main(0), main(npair), main(2 * npair), pl.BlockSpec((tm, lw), lambda h, i: (i, low_blk)),
        prv(0), prv(npair), prv(2 * npair), pl.BlockSpec((8, lw), prev(lambda h: low_blk)),
        vec(0), vec(npair), vec(2 * npair), pl.BlockSpec((1, lw), lambda h, i: (0, low_blk)),
        vec(0), pl.BlockSpec((dr, LANES), lambda h, i: (0, h)),
        vec(0), pl.BlockSpec((ir, LANES), lambda h, i: (0, h)),
        pl.BlockSpec((gr, LANES), lambda h, i: (0, h)),
        vec(0), vec(0), vec(0), vec(0), vec(0),
    ]
    return pl.pallas_call(
        functools.partial(_rwkv_kernel, chunk=chunk, n_chunks=n_chunks, head_dim=head_dim, dr=dr, ir=ir),
        grid=(npair, s // tm),
        in_specs=in_specs,
        out_specs=pl.BlockSpec((tm, LANES), lambda h, i: (i, h)),
        out_shape=jax.ShapeDtypeStruct((s, cw), BF16),
        scratch_shapes=[pltpu.VMEM((LANES, LANES), F32)],
        compiler_params=_params(("parallel", "arbitrary")),
        name="rwkv7_mix",
    )(p, p, p, p, p, p, p, p, mu, mu, mu, mu, w0, w2, a0, a2, g2, k_k, k_a, r_k, gn_g, gn_b)


def _layer_norm_rows(z, g, b):
    mu = jnp.mean(z, axis=-1, keepdims=True)
    zc = z - mu
    var = jnp.mean(zc * zc, axis=-1, keepdims=True)
    return zc * lax.rsqrt(var + LN_EPS) * g + b


def _ln1_kernel(x_ref, mix_ref, gate_ref, g_ref, b_ref, sc_ref, sh_ref, rwh_ref, rwl_ref, rb_ref,
                x1_ref, h_ref, lg_ref, *, alpha):
    z = alpha * x_ref[...] + gate_ref[...] * mix_ref[...].astype(F32)
    x1 = _layer_norm_rows(z, g_ref[...], b_ref[...])
    x1_ref[...] = x1
    h = x1 * (1.0 + sc_ref[...]) + sh_ref[...]
    tm, d = h.shape
    nc = d // LANES
    for c in range(nc):
        h_ref[pl.ds(c, tm, stride=nc), :] = h[:, c * LANES:(c + 1) * LANES]
    hi, lo = _split2(h)
    wh = rwh_ref[...]
    wl = rwl_ref[...]
    lg_ref[...] = (jnp.dot(hi, wh, preferred_element_type=F32) + jnp.dot(lo, wh, preferred_element_type=F32)
                   + jnp.dot(hi, wl, preferred_element_type=F32)) + rb_ref[...]


def _ln1_router(x, mix, gate, g, b, scale, shift, rw_hi, rw_lo, rb, alpha):
    s, d = x.shape
    ne = rw_hi.shape[1]
    tm = _pick(s, 256)
    row = pl.BlockSpec((1, d), lambda i: (0, 0))
    tile = pl.BlockSpec((tm, d), lambda i: (i, 0))
    wspec = pl.BlockSpec((d, ne), lambda i: (0, 0))
    return pl.pallas_call(
        functools.partial(_ln1_kernel, alpha=alpha),
        grid=(s // tm,),
        in_specs=[tile, tile, row, row, row, row, row, wspec, wspec, pl.BlockSpec((1, ne), lambda i: (0, 0))],
        out_specs=[tile, pl.BlockSpec((tm * (d // LANES), LANES), lambda i: (i, 0)),
                   pl.BlockSpec((tm, ne), lambda i: (i, 0))],
        out_shape=[jax.ShapeDtypeStruct((s, d), F32), jax.ShapeDtypeStruct((s * (d // LANES), LANES), F32),
                   jax.ShapeDtypeStruct((s, ne), F32)],
        compiler_params=_params(("parallel",)),
        name="ln1_router",
    )(x, mix, gate, g, b, scale, shift, rw_hi, rw_lo, rb)


def _slab_pitch(nc):
    p = -(-nc // SUBLANES)
    return SUBLANES * (p if p % 2 else p + 1)


def _gather_kernel(tok_ref, base_ref, na_ref, h_hbm, o_ref, buf, sem, *, tm, nc, pitch, unroll):
    b = pl.program_id(0)
    base = base_ref[b]
    last = tok_ref.shape[0] - 1

    def slab_copy(r, tok):
        src = h_hbm.at[pl.ds(pl.multiple_of(tok * nc, SUBLANES), nc)]
        dst = buf.at[pl.ds(pl.multiple_of(r * pitch, SUBLANES), nc)]
        return pltpu.make_async_copy(src, dst, sem)

    def start(r0, carry):
        for u in range(unroll):
            r = r0 * unroll + u
            slab_copy(r, tok_ref[jnp.minimum(base + r, last)]).start(priority=u % 2)
        return carry

    def wait(r0, carry):
        for u in range(unroll):
            slab_copy(r0 * unroll + u, 0).wait()
        return carry

    @pl.when(b < na_ref[0])
    def _():
        lax.fori_loop(0, tm // unroll, start, 0)
        lax.fori_loop(0, tm // unroll, wait, 0)
        for c in range(nc):
            o_ref[:, c * LANES:(c + 1) * LANES] = buf[pl.ds(c, tm, stride=pitch), :].astype(o_ref.dtype)

    @pl.when(b >= na_ref[0])
    def _():
        o_ref[...] = jnp.zeros(o_ref.shape, o_ref.dtype)


def _gather_rows(h_slab, sorted_tok, src_base, n_active, tm, d):
    n_blocks = src_base.shape[0]
    nc = d // LANES
    assert nc % SUBLANES == 0 and h_slab.shape[1] == LANES
    pitch = _slab_pitch(nc)
    return pl.pallas_call(
        functools.partial(_gather_kernel, tm=tm, nc=nc, pitch=pitch, unroll=8),
        grid_spec=pltpu.PrefetchScalarGridSpec(
            num_scalar_prefetch=3,
            grid=(n_blocks,),
            in_specs=[pl.BlockSpec(memory_space=pl.ANY)],
            out_specs=pl.BlockSpec((tm, d), lambda b, tok, base, na: (b, 0)),
            scratch_shapes=[pltpu.VMEM((tm * pitch, LANES), h_slab.dtype), pltpu.SemaphoreType.DMA(())]),
        out_shape=jax.ShapeDtypeStruct((n_blocks * tm, d), BF16),
        compiler_params=_params(("arbitrary",)),
        name="moe_gather",
    )(sorted_tok, src_base, n_active, h_slab)


def _expert_kernel(be_ref, na_ref, x_ref, wg_ref, wu_ref, bg_ref, bu_ref, wd_ref, bd_ref, o_ref, *, nf):
    b = pl.program_id(0)
    f = pl.program_id(1)
    active = b < na_ref[0]

    @pl.when(f == 0)
    def _():
        o_ref[...] = jnp.zeros(o_ref.shape, F32)

    @pl.when(active)
    def _():
        x = x_ref[...]
        gate = jnp.dot(x, wg_ref[...], preferred_element_type=F32) + bg_ref[...]
        up = jnp.dot(x, wu_ref[...], preferred_element_type=F32) + bu_ref[...]
        gate = jnp.minimum(gate, SWIGLU_LIMIT)
        up = jnp.clip(up, -SWIGLU_LIMIT, SWIGLU_LIMIT)
        act = (up + 1.0) * (gate * jax.nn.sigmoid(gate * SWIGLU_ALPHA))
        o_ref[...] += jnp.dot(act.astype(BF16), wd_ref[...], preferred_element_type=F32)

    @pl.when(active & (f == nf - 1))
    def _():
        o_ref[...] += bd_ref[...]


def _expert_ffn(xg, block_e, n_active, w_gu, b_gu, w_down, b_down, tm):
    p, d = xg.shape
    ne, fdim = w_down.shape[0], w_down.shape[1]
    tf = _pick(fdim, 256, LANES)
    nf = fdim // tf

    def fidx(b, f, na):
        return jnp.where(b < na[0], f, nf - 1)

    in_specs = [
        pl.BlockSpec((tm, d), lambda b, f, be, na: (b, 0)),
        pl.BlockSpec((None, d, tf), lambda b, f, be, na: (be[b], 0, fidx(b, f, na))),
        pl.BlockSpec((None, d, tf), lambda b, f, be, na: (be[b], 0, nf + fidx(b, f, na))),
        pl.BlockSpec((None, 1, tf), lambda b, f, be, na: (be[b], 0, fidx(b, f, na))),
        pl.BlockSpec((None, 1, tf), lambda b, f, be, na: (be[b], 0, nf + fidx(b, f, na))),
        pl.BlockSpec((None, tf, d), lambda b, f, be, na: (be[b], fidx(b, f, na), 0)),
        pl.BlockSpec((None, 1, d), lambda b, f, be, na: (be[b], 0, 0)),
    ]
    return pl.pallas_call(
        functools.partial(_expert_kernel, nf=nf),
        grid_spec=pltpu.PrefetchScalarGridSpec(
            num_scalar_prefetch=2,
            grid=(p // tm, nf),
            in_specs=in_specs,
            out_specs=pl.BlockSpec((tm, d), lambda b, f, be, na: (b, 0))),
        out_shape=jax.ShapeDtypeStruct((p, d), F32),
        compiler_params=_params(("arbitrary", "arbitrary")),
        name="moe_experts",
    )(block_e, n_active, xg, w_gu, w_gu, b_gu, b_gu, w_down, b_down)


def _combine_kernel(pos_ref, y_hbm, x1_ref, gw_ref, gate_ref, g_ref, b_ref, o_ref, buf, sem, *, tm, alpha):
    base = pl.program_id(0) * tm * TOP_K
    unroll = 2

    def row_copy(r, kk, src):
        return pltpu.make_async_copy(y_hbm.at[pl.ds(src, 1)], buf.at[kk, pl.ds(r, 1)], sem)

    def start(r0, carry):
        for u in range(unroll):
            r = r0 * unroll + u
            for kk in range(TOP_K):
                row_copy(r, kk, pos_ref[base + r * TOP_K + kk]).start(priority=kk % 2)
        return carry

    def wait(r0, carry):
        for u in range(unroll):
            for kk in range(TOP_K):
                row_copy(r0 * unroll + u, kk, 0).wait()
        return carry

    lax.fori_loop(0, tm // unroll, start, 0)
    lax.fori_loop(0, tm // unroll, wait, 0)
    gw = gw_ref[...]
    ff = buf[0] * gw[:, 0:1]
    for kk in range(1, TOP_K):
        ff = ff + buf[kk] * gw[:, kk:kk + 1]
    z = alpha * x1_ref[...] + gate_ref[...] * ff
    o_ref[...] = _layer_norm_rows(z, g_ref[...], b_ref[...])


def _combine_ln2(y, pos, x1, gate_w, gate, g, b, alpha):
    s, d = x1.shape
    tm = _pick(s, 128)
    row = pl.BlockSpec((1, d), lambda i, pos: (0, 0))
    tile = pl.BlockSpec((tm, d), lambda i, pos: (i, 0))
    return pl.pallas_call(
        functools.partial(_combine_kernel, tm=tm, alpha=alpha),
        grid_spec=pltpu.PrefetchScalarGridSpec(
            num_scalar_prefetch=1,
            grid=(s // tm,),
            in_specs=[pl.BlockSpec(memory_space=pl.ANY), tile,
                      pl.BlockSpec((tm, TOP_K), lambda i, pos: (i, 0)), row, row, row],
            out_specs=tile,
            scratch_shapes=[pltpu.VMEM((TOP_K, tm, d), F32), pltpu.SemaphoreType.DMA(())]),
        out_shape=jax.ShapeDtypeStruct((s, d), F32),
        compiler_params=_params(("arbitrary",)),
        name="moe_combine_ln2",
    )(pos, y, x1, gate_w, gate, g, b)


def _routing_plan(logits, n_experts, tm):
    t = logits.shape[0]
    i32 = jnp.int32
    top_vals, top_idx = lax.top_k(logits, TOP_K)
    gate_w = jax.nn.softmax(top_vals, axis=-1)
    tk = t * TOP_K
    flat_e = top_idx.reshape(tk).astype(i32)
    iota = jnp.arange(tk, dtype=i32)
    sorted_e, order = lax.sort((flat_e, iota), num_keys=1, is_stable=True)
    experts = jnp.arange(n_experts, dtype=i32)
    counts = jnp.sum((flat_e[:, None] == experts[None, :]).astype(i32), axis=0)
    padded = ((counts + tm - 1) // tm) * tm
    start = jnp.cumsum(counts) - counts
    pad_end = jnp.cumsum(padded)
    pad_start = pad_end - padded
    shift_e = pad_start - start
    dest = iota + jnp.sum(jnp.where(sorted_e[:, None] == experts[None, :], shift_e[None, :], 0), axis=1)
    _, pos = lax.sort((order, dest), num_keys=1)
    n_blocks = -(-tk // tm) + n_experts
    blk_row0 = jnp.arange(n_blocks, dtype=i32) * tm
    block_e = jnp.minimum(jnp.sum((pad_end[None, :] <= blk_row0[:, None]).astype(i32), axis=1), n_experts - 1)
    src_base = blk_row0 - shift_e[block_e]
    n_active = (pad_end[-1:] // tm).astype(i32)
    return gate_w, (order // TOP_K).astype(i32), src_base.astype(i32), pos, block_e, n_active


def _layer(x, c_col, prm):
    s, d = x.shape
    n_att_heads = prm["fox_f_bias"].shape[0]
    att_w = prm["w_up_att"].shape[0]
    dh = att_w // n_att_heads
    n_rw_heads, rw_hd = prm["rwkv_r_k"].shape
    rc = prm["rwkv_mu"].shape[0]
    n_experts = prm["router_w"].shape[1]
    depth_alpha = prm["alpha"]

    ada = _ada(c_col, prm["w_ada"], prm["b_ada"][None, :])
    mods = [ada[:, m * d:(m + 1) * d] for m in range(6)]
    shift_m, scale_m, gate_m, shift_f, scale_f, gate_f = mods

    h = _modulate(x, scale_m, shift_m)
    w_in = prm["w_in"]
    off_f = 3 * att_w
    off_rw = off_f + n_att_heads
    off_gate = off_rw + rc
    w_qkv = w_in[:, :off_f].astype(BF16)
    w_f = jnp.pad(w_in[:, off_f:off_rw], ((0, 0), (0, LANES - n_att_heads))).astype(BF16)
    w_rw = w_in[:, off_rw:off_gate].astype(BF16)
    w_gt = w_in[:, off_gate:].astype(BF16)
    q_scale = jnp.concatenate([jnp.full((1, att_w), dh ** -0.5 * LOG2E, F32), jnp.ones((1, 2 * att_w), F32)],
                              axis=1)
    qkv = _matmul(h, w_qkv, BF16, col_scale=q_scale, name="proj_qkv")
    f_logit = _matmul(h, w_f, F32, name="proj_forget")
    p_rw = _matmul(h, w_rw, F32, name="proj_rwkv")
    gates = _matmul(h, w_gt, BF16, act="sigmoid", name="proj_gates")

    cum = _forget_cumsum(f_logit[:, :n_att_heads].T, prm["fox_f_bias"][:, None])
    att = _fox_attention(qkv, cum[:, None, :], n_att_heads, dh)

    row = lambda v: v.reshape(1, -1)
    rw = _rwkv_mix(p_rw, row(prm["rwkv_mu"]), row(prm["rwkv_w0"]), prm["rwkv_w2"].astype(BF16),
                   row(prm["rwkv_a0"]), prm["rwkv_a2"].astype(BF16), prm["rwkv_g2"].astype(BF16),
                   row(prm["rwkv_k_k"]), row(prm["rwkv_k_a"]), row(prm["rwkv_r_k"]),
                   row(prm["rwkv_gn_g"]), row(prm["rwkv_gn_b"]), rw_hd)

    merged = _merge(att, rw, prm["w_up_att"].astype(BF16), prm["w_up_rwkv"].astype(BF16), gates, d)
    mix = _matmul(merged, prm["w_o"].astype(BF16), F32, name="proj_out")

    ne_pad = -(-n_experts // LANES) * LANES
    rw_full = jnp.pad(prm["router_w"], ((0, 0), (0, ne_pad - n_experts)))
    rw_hi = rw_full.astype(BF16)
    rw_lo = (rw_full - rw_hi.astype(F32)).astype(BF16)
    rb = jnp.pad(prm["router_b"], (0, ne_pad - n_experts))[None, :]
    x1, h2, logits = _ln1_router(x, mix, gate_m, row(prm["ln1_g"]), row(prm["ln1_b"]), scale_f, shift_f,
                                 rw_hi, rw_lo, rb, depth_alpha)

    tm_e = _pick(s, 512)
    gate_w, sorted_tok, src_base, pos, block_e, n_active = _routing_plan(logits[:, :n_experts], n_experts, tm_e)
    xg = _gather_rows(h2, sorted_tok, src_base, n_active, tm_e, d)
    y = _expert_ffn(xg, block_e, n_active, prm["w_gate_up"].astype(BF16), prm["b_gate_up"][:, None, :],
                    prm["w_down"].astype(BF16), prm["b_down"][:, None, :], tm_e)
    return _combine_ln2(y, pos, x1, gate_w, gate_f, row(prm["ln2_g"]), row(prm["ln2_b"]), depth_alpha)


def kernel(x, c, w_ada, b_ada, w_in, fox_f_bias, rwkv_mu, rwkv_w0, rwkv_w2, rwkv_a0, rwkv_a2, rwkv_g2,
           rwkv_k_k, rwkv_k_a, rwkv_r_k, rwkv_gn_g, rwkv_gn_b, w_up_att, w_up_rwkv, w_o, ln1_g, ln1_b,
           router_w, router_b, w_gate_up, b_gate_up, w_down, b_down, ln2_g, ln2_b):
    stacked = dict(w_ada=w_ada, b_ada=b_ada, w_in=w_in, fox_f_bias=fox_f_bias, rwkv_mu=rwkv_mu,
                   rwkv_w0=rwkv_w0, rwkv_w2=rwkv_w2, rwkv_a0=rwkv_a0, rwkv_a2=rwkv_a2, rwkv_g2=rwkv_g2,
                   rwkv_k_k=rwkv_k_k, rwkv_k_a=rwkv_k_a, rwkv_r_k=rwkv_r_k, rwkv_gn_g=rwkv_gn_g,
                   rwkv_gn_b=rwkv_gn_b, w_up_att=w_up_att, w_up_rwkv=w_up_rwkv, w_o=w_o, ln1_g=ln1_g,
                   ln1_b=ln1_b, router_w=router_w, router_b=router_b, w_gate_up=w_gate_up,
                   b_gate_up=b_gate_up, w_down=w_down, b_down=b_down, ln2_g=ln2_g, ln2_b=ln2_b)
    depth = w_ada.shape[0]
    alpha = (2 * depth) ** 0.25
    outs = []
    for bi in range(x.shape[0]):
        xb = x[bi]
        c_col = c[bi][:, None]
        for layer in range(depth):
            prm = {name: val[layer] for name, val in stacked.items()}
            prm["alpha"] = alpha
            xb = _layer(xb, c_col, prm)
        outs.append(xb)
    return jnp.stack(outs, axis=0)
```

```python
import functools

import jax
import jax.numpy as jnp
from jax import lax
from jax.experimental import pallas as pl
from jax.experimental.pallas import tpu as pltpu

F32 = jnp.float32
BF16 = jnp.bfloat16

TOP_K = 4
SWIGLU_LIMIT = 7.0
SWIGLU_ALPHA = 1.702
LN_EPS = 1e-5
RWKV_GN_EPS = 64e-5

LANES = 128
SUBLANES = 8
VMEM_LIMIT_BYTES = 56 * 1024 * 1024
RWKV_CHUNK = 64
RWKV_TILE_CHUNKS = 16
NEG_INF = float("-inf")


def _pick(n, pref, align=8):
    if n <= pref:
        return n
    t = (pref // align) * align
    while t >= align:
        if n % t == 0:
            return t
        t -= align
    return n


def _params(sem):
    return pltpu.CompilerParams(dimension_semantics=sem, vmem_limit_bytes=VMEM_LIMIT_BYTES)


def _bdot(a, b):
    return jnp.dot(a.astype(BF16), b.astype(BF16), preferred_element_type=F32)


def _bdot_nt(a, b):
    return lax.dot_general(a.astype(BF16), b.astype(BF16), (((1,), (1,)), ((), ())),
                           preferred_element_type=F32)


def _bdot_tn(a, b):
    return lax.dot_general(a.astype(BF16), b.astype(BF16), (((0,), (0,)), ((), ())),
                           preferred_element_type=F32)


def _split2(x):
    hi = x.astype(BF16)
    lo = (x - hi.astype(F32)).astype(BF16)
    return hi, lo


def _dot_hl(x, m):
    hi, lo = _split2(x)
    return jnp.dot(hi, m, preferred_element_type=F32) + jnp.dot(lo, m, preferred_element_type=F32)


def _dot_lh3(m, x):
    hi = x.astype(BF16)
    r1 = x - hi.astype(F32)
    mid = r1.astype(BF16)
    lo = (r1 - mid.astype(F32)).astype(BF16)
    return (jnp.dot(m, hi, preferred_element_type=F32) + jnp.dot(m, mid, preferred_element_type=F32)
            + jnp.dot(m, lo, preferred_element_type=F32))


def _softplus(u):
    return jnp.maximum(u, 0.0) + jnp.log(1.0 + jnp.exp(-jnp.abs(u)))


def _ada_kernel(c_ref, w_ref, b_ref, o_ref):
    cv = c_ref[...]
    cond = cv * jax.nn.sigmoid(cv)
    o_ref[...] = jnp.sum(w_ref[...] * cond, axis=0, keepdims=True) + b_ref[...]


def _ada(c_col, w, b_row):
    d, n = w.shape
    tn = _pick(n, 512, LANES)
    return pl.pallas_call(
        _ada_kernel,
        grid=(n // tn,),
        in_specs=[pl.BlockSpec((d, 1), lambda j: (0, 0)),
                  pl.BlockSpec((d, tn), lambda j: (0, j)),
                  pl.BlockSpec((1, tn), lambda j: (0, j))],
        out_specs=pl.BlockSpec((1, tn), lambda j: (0, j)),
        out_shape=jax.ShapeDtypeStruct((1, n), F32),
        compiler_params=_params(("parallel",)),
        name="ada_matvec",
    )(c_col, w, b_row)


def _modulate_kernel(x_ref, sc_ref, sh_ref, o_ref):
    o_ref[...] = (x_ref[...] * (1.0 + sc_ref[...]) + sh_ref[...]).astype(o_ref.dtype)


def _modulate(x, scale, shift):
    s, d = x.shape
    tm = _pick(s, 512)
    row = pl.BlockSpec((1, d), lambda i: (0, 0))
    return pl.pallas_call(
        _modulate_kernel,
        grid=(s // tm,),
        in_specs=[pl.BlockSpec((tm, d), lambda i: (i, 0)), row, row],
        out_specs=pl.BlockSpec((tm, d), lambda i: (i, 0)),
        out_shape=jax.ShapeDtypeStruct((s, d), BF16),
        compiler_params=_params(("parallel",)),
        name="modulate",
    )(x, scale, shift)


def _mm_kernel(a_ref, w_ref, *rest, has_scale, act):
    o_ref = rest[-1]
    acc = jnp.dot(a_ref[...], w_ref[...], preferred_element_type=F32)
    if has_scale:
        acc = acc * rest[0][...]
    if act == "sigmoid":
        acc = jax.nn.sigmoid(acc)
    o_ref[...] = acc.astype(o_ref.dtype)


def _matmul(a, w, out_dtype, col_scale=None, act=None, tm_pref=1024, tn_pref=512, name="matmul"):
    m, k = a.shape
    n = w.shape[1]
    tm = _pick(m, tm_pref)
    tn = _pick(n, tn_pref, LANES)
    in_specs = [pl.BlockSpec((tm, k), lambda i, j: (i, 0)),
                pl.BlockSpec((k, tn), lambda i, j: (0, j))]
    args = [a, w]
    if col_scale is not None:
        in_specs.append(pl.BlockSpec((1, tn), lambda i, j: (0, j)))
        args.append(col_scale)
    return pl.pallas_call(
        functools.partial(_mm_kernel, has_scale=col_scale is not None, act=act),
        grid=(m // tm, n // tn),
        in_specs=in_specs,
        out_specs=pl.BlockSpec((tm, tn), lambda i, j: (i, j)),
        out_shape=jax.ShapeDtypeStruct((m, n), out_dtype),
        compiler_params=_params(("parallel", "arbitrary")),
        name=name,
    )(*args)


def _merge_kernel(att_ref, rw_ref, wa_ref, wr_ref, ga_ref, gr_ref, o_ref):
    ya = jnp.dot(att_ref[...], wa_ref[...], preferred_element_type=F32)
    yr = jnp.dot(rw_ref[...], wr_ref[...], preferred_element_type=F32)
    o_ref[...] = (ga_ref[...].astype(F32) * ya + gr_ref[...].astype(F32) * yr).astype(o_ref.dtype)


def _merge(att, rw, w_up_att, w_up_rwkv, gates, d):
    s, ka = att.shape
    kr = rw.shape[1]
    tm = _pick(s, 1024)
    tn = _pick(d, 512, LANES)
    nj = d // tn
    return pl.pallas_call(
        _merge_kernel,
        grid=(s // tm, nj),
        in_specs=[pl.BlockSpec((tm, ka), lambda i, j: (i, 0)),
                  pl.BlockSpec((tm, kr), lambda i, j: (i, 0)),
                  pl.BlockSpec((ka, tn), lambda i, j: (0, j)),
                  pl.BlockSpec((kr, tn), lambda i, j: (0, j)),
                  pl.BlockSpec((tm, tn), lambda i, j: (i, j)),
                  pl.BlockSpec((tm, tn), lambda i, j: (i, j + nj))],
        out_specs=pl.BlockSpec((tm, tn), lambda i, j: (i, j)),
        out_shape=jax.ShapeDtypeStruct((s, d), BF16),
        compiler_params=_params(("parallel", "arbitrary")),
        name="branch_merge",
    )(att, rw, w_up_att, w_up_rwkv, gates, gates)


def _cumsum_kernel(f_ref, b_ref, o_ref, *, n_chunks):
    h = f_ref.shape[0]
    row = lax.broadcasted_iota(jnp.int32, (LANES, LANES), 0)
    col = lax.broadcasted_iota(jnp.int32, (LANES, LANES), 1)
    upper = (row <= col).astype(BF16)
    bias = b_ref[...]

    def body(c, carry):
        off = pl.multiple_of(c * LANES, LANES)
        z = f_ref[:, pl.ds(off, LANES)] + bias
        log_f = -_softplus(-z)
        hi = log_f.astype(BF16)
        r1 = log_f - hi.astype(F32)
        mid = r1.astype(BF16)
        lo = (r1 - mid.astype(F32)).astype(BF16)
        cs = (jnp.dot(hi, upper, preferred_element_type=F32)
              + jnp.dot(mid, upper, preferred_element_type=F32)
              + jnp.dot(lo, upper, preferred_element_type=F32)) + carry
        o_ref[:, pl.ds(off, LANES)] = cs
        return cs[:, LANES - 1:LANES]

    lax.fori_loop(0, n_chunks, body, jnp.zeros((h, 1), F32))


def _forget_cumsum(f_t, bias_col):
    h, s = f_t.shape
    assert s % LANES == 0
    return pl.pallas_call(
        functools.partial(_cumsum_kernel, n_chunks=s // LANES),
        grid=(1,),
        in_specs=[pl.BlockSpec((h, s), lambda i: (0, 0)), pl.BlockSpec((h, 1), lambda i: (0, 0))],
        out_specs=pl.BlockSpec((h, s), lambda i: (0, 0)),
        out_shape=jax.ShapeDtypeStruct((h, s), F32),
        compiler_params=_params(("arbitrary",)),
        name="forget_cumsum",
    )(f_t, bias_col)


LOG2E = 1.4426950408889634
FOX_BLOCK = 1024


def _fox_kernel(q_ref, k_ref, v_ref, ck_ref, o_ref, m_sc, l_sc, acc_sc, *, tq):
    i = pl.program_id(1)
    nsub = tq // LANES
    m_sc[...] = jnp.full(m_sc.shape, NEG_INF, F32)
    l_sc[...] = jnp.zeros(l_sc.shape, F32)
    acc_sc[...] = jnp.zeros(acc_sc.shape, F32)
    q = q_ref[...]
    cq_parts = []
    for rblk in range(nsub):
        roff = pl.multiple_of(i * tq + rblk * LANES, LANES)
        rowv = ck_ref[:, pl.ds(roff, LANES)] * LOG2E
        cq_parts.append(jnp.transpose(jnp.broadcast_to(rowv, (LANES, LANES))))
    cq = jnp.concatenate(cq_parts, axis=0)

    def step(j, masked):
        off = pl.multiple_of(j * tq, tq)
        k = k_ref[pl.ds(off, tq), :]
        v = v_ref[pl.ds(off, tq), :]
        s = lax.dot_general(q, k, (((1,), (1,)), ((), ())), preferred_element_type=F32)
        ck = ck_ref[:, pl.ds(off, tq)] * LOG2E
        ts = []
        for c in range(nsub):
            t = s[:, c * LANES:(c + 1) * LANES] - ck[:, c * LANES:(c + 1) * LANES]
            if masked:
                r = lax.broadcasted_iota(jnp.int32, (tq, LANES), 0)
                cc = lax.broadcasted_iota(jnp.int32, (tq, LANES), 1) + c * LANES
                t = jnp.where(cc <= r, t, NEG_INF)
            ts.append(t)
        mloc = ts[0]
        for c in range(1, nsub):
            mloc = jnp.maximum(mloc, ts[c])
        m_prev = m_sc[...]
        m_new = jnp.maximum(m_prev, jnp.max(mloc, axis=1, keepdims=True) + cq)
        alpha = jnp.exp2(m_prev - m_new)
        shift = m_new - cq
        ps = [jnp.exp2(t - shift) for t in ts]
        lsum = ps[0]
        for c in range(1, nsub):
            lsum = lsum + ps[c]
        p = jnp.concatenate([x.astype(v.dtype) for x in ps], axis=1)
        l_sc[...] = alpha * l_sc[...] + lsum
        acc_sc[...] = alpha * acc_sc[...] + jnp.dot(p, v, preferred_element_type=F32)
        m_sc[...] = m_new

    def body(j, carry):
        step(j, False)
        return carry

    lax.fori_loop(0, i, body, 0)
    step(i, True)
    l_tot = jnp.sum(l_sc[...], axis=1, keepdims=True)
    o_ref[...] = (acc_sc[...] / l_tot).astype(o_ref.dtype)


def _fox_attention(qkv, cum_k, n_heads, dh):
    s = qkv.shape[0]
    assert dh == LANES
    tq = _pick(s, FOX_BLOCK, LANES)
    nq = s // tq
    return pl.pallas_call(
        functools.partial(_fox_kernel, tq=tq),
        grid=(n_heads, nq),
        in_specs=[pl.BlockSpec((tq, dh), lambda h, i: (i, h)),
                  pl.BlockSpec((s, dh), lambda h, i: (0, n_heads + h)),
                  pl.BlockSpec((s, dh), lambda h, i: (0, 2 * n_heads + h)),
                  pl.BlockSpec((None, 1, s), lambda h, i: (h, 0, 0))],
        out_specs=pl.BlockSpec((tq, dh), lambda h, i: (i, h)),
        out_shape=jax.ShapeDtypeStruct((s, n_heads * dh), BF16),
        scratch_shapes=[pltpu.VMEM((tq, LANES), F32), pltpu.VMEM((tq, LANES), F32), pltpu.VMEM((tq, dh), F32)],
        compiler_params=_params(("parallel", "arbitrary")),
        name="fox_attention",
    )(qkv, qkv, qkv, cum_k)


def _shifted(p_ref, pp_ref, mu_ref, first):
    p = p_ref[...]
    rows = p.shape[0]
    prev_last = jnp.where(first, 0.0, pp_ref[7:8, :])
    rolled = pltpu.roll(p, 1, 0)
    r = lax.broadcasted_iota(jnp.int32, (rows, 1), 0)
    p_prev = jnp.where(r == 0, prev_last, rolled)
    return p + (p_prev - p) * mu_ref[...]


INV_LEAF = 8


def _unit_lower_inverse(lm, sr, sc, n, eye):
    same = lambda b: (sr // b) == (sc // b)
    leaf = min(INV_LEAF, n)
    l0 = jnp.where(same(leaf), lm, 0.0)
    inv = eye - l0
    pw = l0
    span = 2
    while span < leaf:
        pw = _bmm(pw, pw)
        inv = inv + _bmm(inv, pw)
        span *= 2
    b = leaf
    while b < n:
        off = jnp.where(same(2 * b) & jnp.logical_not(same(b)), lm, 0.0)
        inv = inv - _bmm(inv, _bmm(off, inv))
        b *= 2
    return inv


def _bmm(a, b):
    return lax.dot_general(a.astype(BF16), b.astype(BF16), (((2,), (1,)), ((0,), (0,))),
                           preferred_element_type=F32)


def _bmm_nt(a, b):
    return lax.dot_general(a.astype(BF16), b.astype(BF16), (((2,), (2,)), ((0,), (0,))),
                           preferred_element_type=F32)


def _rwkv_kernel(pr_ref, pk_ref, pv_ref, pl_ref, ppr_ref, ppk_ref, ppv_ref, ppl_ref,
                 mur_ref, muk_ref, muv_ref, mul_ref,
                 w0_ref, w2_ref, a0_ref, a2_ref, g2_ref, kk_ref, ka_ref, rk_ref, gng_ref, gnb_ref,
                 o_ref, state_sc, *, chunk, n_chunks, head_dim, dr, ir):
    i = pl.program_id(1)
    first = i == 0

    @pl.when(first)
    def _():
        state_sc[...] = jnp.zeros(state_sc.shape, F32)

    r = _shifted(pr_ref, ppr_ref, mur_ref, first)
    k = _shifted(pk_ref, ppk_ref, muk_ref, first)
    v = _shifted(pv_ref, ppv_ref, muv_ref, first)
    low = _shifted(pl_ref, ppl_ref, mul_ref, first)
    wd = low[:, :dr]
    ad = low[:, dr:dr + ir]
    gd = low[:, dr + ir:]

    z = w0_ref[...] + _bdot(jnp.tanh(wd), w2_ref[...])
    log_w = -jnp.exp(-_softplus(-z) - 0.5)
    a = jax.nn.sigmoid(a0_ref[...] + _bdot(ad, a2_ref[...]))
    g = _bdot(jax.nn.sigmoid(gd), g2_ref[...])

    lane_r = lax.broadcasted_iota(jnp.int32, (LANES, LANES), 0)
    lane_c = lax.broadcasted_iota(jnp.int32, (LANES, LANES), 1)
    same_head = (lane_r // head_dim) == (lane_c // head_dim)
    ones_bd = same_head.astype(BF16)

    kk_raw = k * kk_ref[...]
    ss = _dot_hl(kk_raw * kk_raw, ones_bd)
    kappa = kk_raw / jnp.maximum(jnp.sqrt(ss), 1e-12)
    k_mod = k * (1.0 + (a - 1.0) * ka_ref[...])
    beta = kappa * a

    c2 = 2 * chunk
    lane = lax.broadcasted_iota(jnp.int32, (1, LANES), 1)
    head0 = lane < head_dim
    tr = lax.broadcasted_iota(jnp.int32, (chunk, chunk), 0)
    tc = lax.broadcasted_iota(jnp.int32, (chunk, chunk), 1)
    tri = (tc <= tr).astype(BF16)
    sr = lax.broadcasted_iota(jnp.int32, (c2, c2), 0)
    sc = lax.broadcasted_iota(jnp.int32, (c2, c2), 1)
    same_blk = (sr // chunk) == (sc // chunk)
    strict = same_blk & ((sc % chunk) < (sr % chunk))
    incl = same_blk & ((sc % chunk) <= (sr % chunk))
    eye = (sr == sc).astype(F32)

    ng = n_chunks
    tm = ng * chunk

    def chunked(x):
        return x.reshape(ng, chunk, LANES)

    def stack(x):
        return jnp.concatenate([jnp.where(head0, x, 0.0), jnp.where(head0, 0.0, x)], axis=1)

    lw = chunked(log_w)
    tri_b = jnp.broadcast_to(tri, (ng, chunk, chunk))
    hi = lw.astype(BF16)
    r1 = lw - hi.astype(F32)
    mid = r1.astype(BF16)
    lo = (r1 - mid.astype(F32)).astype(BF16)
    cum = _bmm(tri_b, hi) + _bmm(tri_b, mid) + _bmm(tri_b, lo)
    e_inc = jnp.exp(cum)
    e_neg = jnp.exp(-cum)
    e_exc = jnp.exp(cum - lw)
    rs = stack(chunked(r) * e_inc)
    ks = stack(chunked(k_mod) * e_neg)
    bs = stack(chunked(beta) * e_neg)
    kps = stack(chunked(kappa) * e_exc)
    vs = stack(chunked(v))
    prod = _bmm_nt(jnp.concatenate([kps, rs], axis=1), jnp.concatenate([ks, bs], axis=1))
    a_kk = jnp.where(strict, prod[:, :c2, :c2], 0.0)
    lm = jnp.where(strict, prod[:, :c2, c2:], 0.0)
    a_rk = jnp.where(incl, prod[:, c2:, :c2], 0.0)
    a_rb = jnp.where(incl, prod[:, c2:, c2:], 0.0)
    tinv = _unit_lower_inverse(lm, sr, sc, chunk, eye)
    av = _bmm(jnp.concatenate([a_kk, a_rk], axis=1), vs)
    wu = _bmm(tinv, jnp.concatenate([kps, av[:, :c2]], axis=2))
    rbw = _bmm(a_rb, wu)
    qe = rs - rbw[:, :, :LANES]
    y0 = av[:, c2:] - rbw[:, :, LANES:]
    mp, g0 = [], []
    for c in range(ng):
        tb = _bdot_tn(wu[c], bs[c])
        mp.append(tb[:LANES])
        g0.append(_bdot_tn(vs[c], ks[c]) - tb[LANES:])

    ys = []
    state = state_sc[...]
    for c in range(ng):
        y_st = _bdot_nt(qe[c], state) + y0[c]
        ys.append(y_st[:chunk] + y_st[chunk:])
        state = (state - _bdot(state, mp[c]) + g0[c]) * e_inc[c, chunk - 1:chunk, :]
    state_sc[...] = state
    y = jnp.concatenate(ys, axis=0) if ng > 1 else ys[0]
    assert y.shape == (tm, LANES)

    avg_bd = ones_bd * (1.0 / head_dim)
    mean = _dot_hl(y, avg_bd)
    dlt = y - mean
    var = _dot_hl(dlt * dlt, avg_bd)
    yn = dlt * lax.rsqrt(var + RWKV_GN_EPS) * gng_ref[...] + gnb_ref[...]
    bonus = _dot_hl(r * k_mod * rk_ref[...], ones_bd) * v
    o_ref[...] = ((yn + bonus) * g).astype(o_ref.dtype)


def _rwkv_mix(p, mu, w0, w2, a0, a2, g2, k_k, k_a, r_k, gn_g, gn_b, head_dim):
    s, rc = p.shape
    cw = w0.shape[1]
    dr, ir, gr = w2.shape[0], a2.shape[0], g2.shape[0]
    lw = dr + ir + gr
    assert head_dim * 2 == LANES and cw % LANES == 0 and (head_dim & (head_dim - 1)) == 0
    assert (3 * cw) % lw == 0 and rc == 3 * cw + lw
    chunk = min(RWKV_CHUNK, s)
    tm = _pick(s, RWKV_TILE_CHUNKS * chunk, chunk)
    n_chunks = tm // chunk
    npair = cw // LANES
    low_blk = (3 * cw) // lw
    t8 = tm // 8

    def prev(col_fn):
        return lambda h, i: (jnp.maximum(i * t8 - 1, 0), col_fn(h))

    main = lambda off: pl.BlockSpec((tm, LANES), lambda h, i: (i, off + h))
    prv = lambda off: pl.BlockSpec((8, LANES), prev(lambda h: off + h))
    vec = lambda off: pl.BlockSpec((1, LANES), lambda h, i: (0, off + h))
    in_specs = [
        main(0), main(npair), main(2 * npair), pl.BlockSpec((tm, lw), lambda h, i: (i, low_blk)),
        prv(0), prv(npair), prv(2 * npair), pl.BlockSpec((8, lw), prev(lambda h: low_blk)),
        vec(0), vec(npair), vec(2 * npair), pl.BlockSpec((1, lw), lambda h, i: (0, low_blk)),
        vec(0), pl.BlockSpec((dr, LANES), lambda h, i: (0, h)),
        vec(0), pl.BlockSpec((ir, LANES), lambda h, i: (0, h)),
        pl.BlockSpec((gr, LANES), lambda h, i: (0, h)),
        vec(0), vec(0), vec(0), vec(0), vec(0),
    ]
    return pl.pallas_call(
        functools.partial(_rwkv_kernel, chunk=chunk, n_chunks=n_chunks, head_dim=head_dim, dr=dr, ir=ir),
        grid=(npair, s // tm),
        in_specs=in_specs,
        out_specs=pl.BlockSpec((tm, LANES), lambda h, i: (i, h)),
        out_shape=jax.ShapeDtypeStruct((s, cw), BF16),
        scratch_shapes=[pltpu.VMEM((LANES, LANES), F32)],
        compiler_params=_params(("parallel", "arbitrary")),
        name="rwkv7_mix",
    )(p, p, p, p, p, p, p, p, mu, mu, mu, mu, w0, w2, a0, a2, g2, k_k, k_a, r_k, gn_g, gn_b)


def _layer_norm_rows(z, g, b):
    mu = jnp.mean(z, axis=-1, keepdims=True)
    zc = z - mu
    var = jnp.mean(zc * zc, axis=-1, keepdims=True)
    return zc * lax.rsqrt(var + LN_EPS) * g + b


def _ln1_kernel(x_ref, mix_ref, gate_ref, g_ref, b_ref, sc_ref, sh_ref, rwh_ref, rwl_ref, rb_ref,
                x1_ref, h_ref, lg_ref, *, alpha):
    z = alpha * x_ref[...] + gate_ref[...] * mix_ref[...].astype(F32)
    x1 = _layer_norm_rows(z, g_ref[...], b_ref[...])
    x1_ref[...] = x1
    h = x1 * (1.0 + sc_ref[...]) + sh_ref[...]
    tm, d = h.shape
    nc = d // LANES
    for c in range(nc):
        h_ref[pl.ds(c, tm, stride=nc), :] = h[:, c * LANES:(c + 1) * LANES]
    hi, lo = _split2(h)
    wh = rwh_ref[...]
    wl = rwl_ref[...]
    lg_ref[...] = (jnp.dot(hi, wh, preferred_element_type=F32) + jnp.dot(lo, wh, preferred_element_type=F32)
                   + jnp.dot(hi, wl, preferred_element_type=F32)) + rb_ref[...]


def _ln1_router(x, mix, gate, g, b, scale, shift, rw_hi, rw_lo, rb, alpha):
    s, d = x.shape
    ne = rw_hi.shape[1]
    tm = _pick(s, 256)
    row = pl.BlockSpec((1, d), lambda i: (0, 0))
    tile = pl.BlockSpec((tm, d), lambda i: (i, 0))
    wspec = pl.BlockSpec((d, ne), lambda i: (0, 0))
    return pl.pallas_call(
        functools.partial(_ln1_kernel, alpha=alpha),
        grid=(s // tm,),
        in_specs=[tile, tile, row, row, row, row, row, wspec, wspec, pl.BlockSpec((1, ne), lambda i: (0, 0))],
        out_specs=[tile, pl.BlockSpec((tm * (d // LANES), LANES), lambda i: (i, 0)),
                   pl.BlockSpec((tm, ne), lambda i: (i, 0))],
        out_shape=[jax.ShapeDtypeStruct((s, d), F32), jax.ShapeDtypeStruct((s * (d // LANES), LANES), F32),
                   jax.ShapeDtypeStruct((s, ne), F32)],
        compiler_params=_params(("parallel",)),
        name="ln1_router",
    )(x, mix, gate, g, b, scale, shift, rw_hi, rw_lo, rb)


def _slab_pitch(nc):
    p = -(-nc // SUBLANES)
    return SUBLANES * (p if p % 2 else p + 1)


def _gather_kernel(tok_ref, base_ref, na_ref, h_hbm, o_ref, buf, sem, *, tm, nc, pitch, unroll):
    b = pl.program_id(0)
    base = base_ref[b]
    last = tok_ref.shape[0] - 1

    def slab_copy(r, tok):
        src = h_hbm.at[pl.ds(pl.multiple_of(tok * nc, SUBLANES), nc)]
        dst = buf.at[pl.ds(pl.multiple_of(r * pitch, SUBLANES), nc)]
        return pltpu.make_async_copy(src, dst, sem)

    def start(r0, carry):
        for u in range(unroll):
            r = r0 * unroll + u
            slab_copy(r, tok_ref[jnp.minimum(base + r, last)]).start(priority=u % 2)
        return carry

    def wait(r0, carry):
        for u in range(unroll):
            slab_copy(r0 * unroll + u, 0).wait()
        return carry

    @pl.when(b < na_ref[0])
    def _():
        lax.fori_loop(0, tm // unroll, start, 0)
        lax.fori_loop(0, tm // unroll, wait, 0)
        for c in range(nc):
            o_ref[:, c * LANES:(c + 1) * LANES] = buf[pl.ds(c, tm, stride=pitch), :].astype(o_ref.dtype)

    @pl.when(b >= na_ref[0])
    def _():
        o_ref[...] = jnp.zeros(o_ref.shape, o_ref.dtype)


def _gather_rows(h_slab, sorted_tok, src_base, n_active, tm, d):
    n_blocks = src_base.shape[0]
    nc = d // LANES
    assert nc % SUBLANES == 0 and h_slab.shape[1] == LANES
    pitch = _slab_pitch(nc)
    return pl.pallas_call(
        functools.partial(_gather_kernel, tm=tm, nc=nc, pitch=pitch, unroll=8),
        grid_spec=pltpu.PrefetchScalarGridSpec(
            num_scalar_prefetch=3,
            grid=(n_blocks,),
            in_specs=[pl.BlockSpec(memory_space=pl.ANY)],
            out_specs=pl.BlockSpec((tm, d), lambda b, tok, base, na: (b, 0)),
            scratch_shapes=[pltpu.VMEM((tm * pitch, LANES), h_slab.dtype), pltpu.SemaphoreType.DMA(())]),
        out_shape=jax.ShapeDtypeStruct((n_blocks * tm, d), BF16),
        compiler_params=_params(("arbitrary",)),
        name="moe_gather",
    )(sorted_tok, src_base, n_active, h_slab)


def _tile_schedule(block_e, blk_start, nblk, n_active, n_tiles):
    i32 = jnp.int32
    n_steps = block_e.shape[0] * n_tiles
    s = jnp.arange(n_steps, dtype=i32)
    b0 = s // n_tiles
    live = b0 < n_active[0]
    e = block_e[b0]
    nb_e = jnp.maximum(nblk[e], 1)
    local = s - n_tiles * blk_start[e]
    e_last = block_e[jnp.maximum(n_active[0] - 1, 0)]
    sb = jnp.where(live, blk_start[e] + local % nb_e, b0)
    st_out = jnp.where(live, local // nb_e, s % n_tiles)
    st_w = jnp.where(live, local // nb_e, n_tiles - 1)
    se = jnp.where(live, e, e_last)
    key = se * n_tiles + st_w
    fresh = jnp.concatenate([jnp.ones((1,), i32), (key[1:] != key[:-1]).astype(i32)])
    n_live = (n_active * n_tiles).astype(i32)
    return sb.astype(i32), st_out.astype(i32), st_w.astype(i32), se.astype(i32), fresh, n_live


def _gate_up_kernel(sb, so, sw, se, fresh, n_live, x_ref, wg_ref, wu_ref, bg_ref, bu_ref, o_ref, wg_sc, wu_sc):
    s = pl.program_id(0)

    @pl.when(fresh[s] == 1)
    def _():
        wg_sc[...] = wg_ref[...].astype(BF16)
        wu_sc[...] = wu_ref[...].astype(BF16)

    @pl.when(s < n_live[0])
    def _():
        x = x_ref[...]
        gate = jnp.dot(x, wg_sc[...], preferred_element_type=F32) + bg_ref[...]
        up = jnp.dot(x, wu_sc[...], preferred_element_type=F32) + bu_ref[...]
        gate = jnp.minimum(gate, SWIGLU_LIMIT)
        up = jnp.clip(up, -SWIGLU_LIMIT, SWIGLU_LIMIT)
        o_ref[...] = ((up + 1.0) * (gate * jax.nn.sigmoid(gate * SWIGLU_ALPHA))).astype(o_ref.dtype)

    @pl.when(s >= n_live[0])
    def _():
        o_ref[...] = jnp.zeros(o_ref.shape, o_ref.dtype)


def _down_kernel(sb, so, sw, se, fresh, n_live, a_ref, wd_ref, bd_ref, o_ref, wd_sc):
    s = pl.program_id(0)

    @pl.when(fresh[s] == 1)
    def _():
        wd_sc[...] = wd_ref[...].astype(BF16)

    @pl.when(s < n_live[0])
    def _():
        o_ref[...] = jnp.dot(a_ref[...], wd_sc[...], preferred_element_type=F32) + bd_ref[...]

    @pl.when(s >= n_live[0])
    def _():
        o_ref[...] = jnp.zeros(o_ref.shape, o_ref.dtype)


def _expert_ffn(xg, block_e, blk_start, nblk, n_active, w_gu, b_gu, w_down, b_down, tm):
    p, d = xg.shape
    fdim = w_down.shape[1]
    n_blocks = p // tm
    tf = _pick(fdim, 256, LANES)
    nf = fdim // tf
    sched = _tile_schedule(block_e, blk_start, nblk, n_active, nf)
    act = pl.pallas_call(
        _gate_up_kernel,
        grid_spec=pltpu.PrefetchScalarGridSpec(
            num_scalar_prefetch=6,
            grid=(n_blocks * nf,),
            in_specs=[
                pl.BlockSpec((tm, d), lambda s, sb, so, sw, se, fr, nl: (sb[s], 0)),
                pl.BlockSpec((None, d, tf), lambda s, sb, so, sw, se, fr, nl: (se[s], 0, sw[s])),
                pl.BlockSpec((None, d, tf), lambda s, sb, so, sw, se, fr, nl: (se[s], 0, nf + sw[s])),
                pl.BlockSpec((None, 1, tf), lambda s, sb, so, sw, se, fr, nl: (se[s], 0, sw[s])),
                pl.BlockSpec((None, 1, tf), lambda s, sb, so, sw, se, fr, nl: (se[s], 0, nf + sw[s])),
            ],
            out_specs=pl.BlockSpec((tm, tf), lambda s, sb, so, sw, se, fr, nl: (sb[s], so[s])),
            scratch_shapes=[pltpu.VMEM((d, tf), BF16), pltpu.VMEM((d, tf), BF16)]),
        out_shape=jax.ShapeDtypeStruct((p, fdim), BF16),
        compiler_params=_params(("arbitrary",)),
        name="moe_gate_up",
    )(*sched, xg, w_gu, w_gu, b_gu, b_gu)

    tn = _pick(d, 1024, LANES)
    nn = d // tn
    sched = _tile_schedule(block_e, blk_start, nblk, n_active, nn)
    return pl.pallas_call(
        _down_kernel,
        grid_spec=pltpu.PrefetchScalarGridSpec(
            num_scalar_prefetch=6,
            grid=(n_blocks * nn,),
            in_specs=[
                pl.BlockSpec((tm, fdim), lambda s, sb, so, sw, se, fr, nl: (sb[s], 0)),
                pl.BlockSpec((None, fdim, tn), lambda s, sb, so, sw, se, fr, nl: (se[s], 0, sw[s])),
                pl.BlockSpec((None, 1, tn), lambda s, sb, so, sw, se, fr, nl: (se[s], 0, sw[s])),
            ],
            out_specs=pl.BlockSpec((tm, tn), lambda s, sb, so, sw, se, fr, nl: (sb[s], so[s])),
            scratch_shapes=[pltpu.VMEM((fdim, tn), BF16)]),
        out_shape=jax.ShapeDtypeStruct((p, d), F32),
        compiler_params=_params(("arbitrary",)),
        name="moe_down",
    )(*sched, act, w_down, b_down)


def _combine_kernel(pos_ref, y_hbm, x1_ref, gw_ref, gate_ref, g_ref, b_ref, o_ref, buf, sem, *, tm, alpha):
    base = pl.program_id(0) * tm * TOP_K
    unroll = 2

    def row_copy(r, kk, src):
        return pltpu.make_async_copy(y_hbm.at[pl.ds(src, 1)], buf.at[kk, pl.ds(r, 1)], sem)

    def start(r0, carry):
        for u in range(unroll):
            r = r0 * unroll + u
            for kk in range(TOP_K):
                row_copy(r, kk, pos_ref[base + r * TOP_K + kk]).start(priority=kk % 2)
        return carry

    def wait(r0, carry):
        for u in range(unroll):
            for kk in range(TOP_K):
                row_copy(r0 * unroll + u, kk, 0).wait()
        return carry

    lax.fori_loop(0, tm // unroll, start, 0)
    lax.fori_loop(0, tm // unroll, wait, 0)
    gw = gw_ref[...]
    ff = buf[0] * gw[:, 0:1]
    for kk in range(1, TOP_K):
        ff = ff + buf[kk] * gw[:, kk:kk + 1]
    z = alpha * x1_ref[...] + gate_ref[...] * ff
    o_ref[...] = _layer_norm_rows(z, g_ref[...], b_ref[...])


def _combine_ln2(y, pos, x1, gate_w, gate, g, b, alpha):
    s, d = x1.shape
    tm = _pick(s, 128)
    row = pl.BlockSpec((1, d), lambda i, pos: (0, 0))
    tile = pl.BlockSpec((tm, d), lambda i, pos: (i, 0))
    return pl.pallas_call(
        functools.partial(_combine_kernel, tm=tm, alpha=alpha),
        grid_spec=pltpu.PrefetchScalarGridSpec(
            num_scalar_prefetch=1,
            grid=(s // tm,),
            in_specs=[pl.BlockSpec(memory_space=pl.ANY), tile,
                      pl.BlockSpec((tm, TOP_K), lambda i, pos: (i, 0)), row, row, row],
            out_specs=tile,
            scratch_shapes=[pltpu.VMEM((TOP_K, tm, d), F32), pltpu.SemaphoreType.DMA(())]),
        out_shape=jax.ShapeDtypeStruct((s, d), F32),
        compiler_params=_params(("arbitrary",)),
        name="moe_combine_ln2",
    )(pos, y, x1, gate_w, gate, g, b)


def _routing_plan(logits, n_experts, tm):
    t = logits.shape[0]
    i32 = jnp.int32
    top_vals, top_idx = lax.top_k(logits, TOP_K)
    gate_w = jax.nn.softmax(top_vals, axis=-1)
    tk = t * TOP_K
    flat_e = top_idx.reshape(tk).astype(i32)
    iota = jnp.arange(tk, dtype=i32)
    sorted_e, order = lax.sort((flat_e, iota), num_keys=1, is_stable=True)
    experts = jnp.arange(n_experts, dtype=i32)
    counts = jnp.sum((flat_e[:, None] == experts[None, :]).astype(i32), axis=0)
    padded = ((counts + tm - 1) // tm) * tm
    start = jnp.cumsum(counts) - counts
    pad_end = jnp.cumsum(padded)
    pad_start = pad_end - padded
    shift_e = pad_start - start
    dest = iota + jnp.sum(jnp.where(sorted_e[:, None] == experts[None, :], shift_e[None, :], 0), axis=1)
    _, pos = lax.sort((order, dest), num_keys=1)
    n_blocks = -(-tk // tm) + n_experts
    blk_row0 = jnp.arange(n_blocks, dtype=i32) * tm
    block_e = jnp.minimum(jnp.sum((pad_end[None, :] <= blk_row0[:, None]).astype(i32), axis=1), n_experts - 1)
    src_base = blk_row0 - shift_e[block_e]
    n_active = (pad_end[-1:] // tm).astype(i32)
    blocks = (block_e.astype(i32), (pad_start // tm).astype(i32), (padded // tm).astype(i32), n_active)
    return gate_w, (order // TOP_K).astype(i32), src_base.astype(i32), pos, blocks


def _layer(x, c_col, prm):
    s, d = x.shape
    n_att_heads = prm["fox_f_bias"].shape[0]
    att_w = prm["w_up_att"].shape[0]
    dh = att_w // n_att_heads
    n_rw_heads, rw_hd = prm["rwkv_r_k"].shape
    rc = prm["rwkv_mu"].shape[0]
    n_experts = prm["router_w"].shape[1]
    depth_alpha = prm["alpha"]

    ada = _ada(c_col, prm["w_ada"], prm["b_ada"][None, :])
    mods = [ada[:, m * d:(m + 1) * d] for m in range(6)]
    shift_m, scale_m, gate_m, shift_f, scale_f, gate_f = mods

    h = _modulate(x, scale_m, shift_m)
    w_in = prm["w_in"]
    off_f = 3 * att_w
    off_rw = off_f + n_att_heads
    off_gate = off_rw + rc
    w_qkv = w_in[:, :off_f].astype(BF16)
    w_f = jnp.pad(w_in[:, off_f:off_rw], ((0, 0), (0, LANES - n_att_heads))).astype(BF16)
    w_rw = w_in[:, off_rw:off_gate].astype(BF16)
    w_gt = w_in[:, off_gate:].astype(BF16)
    q_scale = jnp.concatenate([jnp.full((1, att_w), dh ** -0.5 * LOG2E, F32), jnp.ones((1, 2 * att_w), F32)],
                              axis=1)
    qkv = _matmul(h, w_qkv, BF16, col_scale=q_scale, name="proj_qkv")
    f_logit = _matmul(h, w_f, F32, name="proj_forget")
    p_rw = _matmul(h, w_rw, F32, name="proj_rwkv")
    gates = _matmul(h, w_gt, BF16, act="sigmoid", name="proj_gates")

    cum = _forget_cumsum(f_logit[:, :n_att_heads].T, prm["fox_f_bias"][:, None])
    att = _fox_attention(qkv, cum[:, None, :], n_att_heads, dh)

    row = lambda v: v.reshape(1, -1)
    rw = _rwkv_mix(p_rw, row(prm["rwkv_mu"]), row(prm["rwkv_w0"]), prm["rwkv_w2"].astype(BF16),
                   row(prm["rwkv_a0"]), prm["rwkv_a2"].astype(BF16), prm["rwkv_g2"].astype(BF16),
                   row(prm["rwkv_k_k"]), row(prm["rwkv_k_a"]), row(prm["rwkv_r_k"]),
                   row(prm["rwkv_gn_g"]), row(prm["rwkv_gn_b"]), rw_hd)

    merged = _merge(att, rw, prm["w_up_att"].astype(BF16), prm["w_up_rwkv"].astype(BF16), gates, d)
    mix = _matmul(merged, prm["w_o"].astype(BF16), F32, name="proj_out")

    ne_pad = -(-n_experts // LANES) * LANES
    rw_full = jnp.pad(prm["router_w"], ((0, 0), (0, ne_pad - n_experts)))
    rw_hi = rw_full.astype(BF16)
    rw_lo = (rw_full - rw_hi.astype(F32)).astype(BF16)
    rb = jnp.pad(prm["router_b"], (0, ne_pad - n_experts))[None, :]
    x1, h2, logits = _ln1_router(x, mix, gate_m, row(prm["ln1_g"]), row(prm["ln1_b"]), scale_f, shift_f,
                                 rw_hi, rw_lo, rb, depth_alpha)

    tm_e = _pick(s, 512)
    gate_w, sorted_tok, src_base, pos, blocks = _routing_plan(logits[:, :n_experts], n_experts, tm_e)
    xg = _gather_rows(h2, sorted_tok, src_base, blocks[3], tm_e, d)
    y = _expert_ffn(xg, *blocks, prm["w_gate_up"], prm["b_gate_up"][:, None, :],
                    prm["w_down"], prm["b_down"][:, None, :], tm_e)
    return _combine_ln2(y, pos, x1, gate_w, gate_f, row(prm["ln2_g"]), row(prm["ln2_b"]), depth_alpha)


def kernel(x, c, w_ada, b_ada, w_in, fox_f_bias, rwkv_mu, rwkv_w0, rwkv_w2, rwkv_a0, rwkv_a2, rwkv_g2,
           rwkv_k_k, rwkv_k_a, rwkv_r_k, rwkv_gn_g, rwkv_gn_b, w_up_att, w_up_rwkv, w_o, ln1_g, ln1_b,
           router_w, router_b, w_gate_up, b_gate_up, w_down, b_down, ln2_g, ln2_b):
    stacked = dict(w_ada=w_ada, b_ada=b_ada, w_in=w_in, fox_f_bias=fox_f_bias, rwkv_mu=rwkv_mu,
                   rwkv_w0=rwkv_w0, rwkv_w2=rwkv_w2, rwkv_a0=rwkv_a0, rwkv_a2=rwkv_a2, rwkv_g2=rwkv_g2,
                   rwkv_k_k=rwkv_k_k, rwkv_k_a=rwkv_k_a, rwkv_r_k=rwkv_r_k, rwkv_gn_g=rwkv_gn_g,
                   rwkv_gn_b=rwkv_gn_b, w_up_att=w_up_att, w_up_rwkv=w_up_rwkv, w_o=w_o, ln1_g=ln1_g,
                   ln1_b=ln1_b, router_w=router_w, router_b=router_b, w_gate_up=w_gate_up,
                   b_gate_up=b_gate_up, w_down=w_down, b_down=b_down, ln2_g=ln2_g, ln2_b=ln2_b)
    depth = w_ada.shape[0]
    alpha = (2 * depth) ** 0.25
    outs = []
    n_batch, seq, d_model = x.shape
    for bi in range(n_batch):
        xb = x.reshape(seq, d_model) if n_batch == 1 else x[bi]
        c_col = c[bi][:, None]
        for layer in range(depth):
            prm = {name: val[layer] for name, val in stacked.items()}
            prm["alpha"] = alpha
            xb = _layer(xb, c_col, prm)
        outs.append(xb)
    if n_batch == 1:
        return outs[0].reshape(1, seq, d_model)
    return jnp.stack(outs, axis=0)
```

```python
import functools

import jax
import jax.numpy as jnp
from jax import lax
from jax.experimental import pallas as pl
from jax.experimental.pallas import tpu as pltpu

F32 = jnp.float32
BF16 = jnp.bfloat16

TOP_K = 4
SWIGLU_LIMIT = 7.0
SWIGLU_ALPHA = 1.702
LN_EPS = 1e-5
RWKV_GN_EPS = 64e-5

LANES = 128
SUBLANES = 8
VMEM_LIMIT_BYTES = 56 * 1024 * 1024
RWKV_CHUNK = 64
RWKV_TILE_CHUNKS = 16
NEG_INF = float("-inf")


def _pick(n, pref, align=8):
    if n <= pref:
        return n
    t = (pref // align) * align
    while t >= align:
        if n % t == 0:
            return t
        t -= align
    return n


def _params(sem):
    return pltpu.CompilerParams(dimension_semantics=sem, vmem_limit_bytes=VMEM_LIMIT_BYTES)


def _bdot(a, b):
    return jnp.dot(a.astype(BF16), b.astype(BF16), preferred_element_type=F32)


def _bdot_nt(a, b):
    return lax.dot_general(a.astype(BF16), b.astype(BF16), (((1,), (1,)), ((), ())),
                           preferred_element_type=F32)


def _bdot_tn(a, b):
    return lax.dot_general(a.astype(BF16), b.astype(BF16), (((0,), (0,)), ((), ())),
                           preferred_element_type=F32)


def _split2(x):
    hi = x.astype(BF16)
    lo = (x - hi.astype(F32)).astype(BF16)
    return hi, lo


def _dot_hl(x, m):
    hi, lo = _split2(x)
    return jnp.dot(hi, m, preferred_element_type=F32) + jnp.dot(lo, m, preferred_element_type=F32)


def _dot_lh3(m, x):
    hi = x.astype(BF16)
    r1 = x - hi.astype(F32)
    mid = r1.astype(BF16)
    lo = (r1 - mid.astype(F32)).astype(BF16)
    return (jnp.dot(m, hi, preferred_element_type=F32) + jnp.dot(m, mid, preferred_element_type=F32)
            + jnp.dot(m, lo, preferred_element_type=F32))


def _softplus(u):
    return jnp.maximum(u, 0.0) + jnp.log(1.0 + jnp.exp(-jnp.abs(u)))


def _ada_kernel(c_ref, w_ref, b_ref, o_ref):
    cv = c_ref[...]
    cond = cv * jax.nn.sigmoid(cv)
    o_ref[...] = jnp.sum(w_ref[...] * cond, axis=0, keepdims=True) + b_ref[...]


def _ada(c_col, w, b_row):
    d, n = w.shape
    tn = _pick(n, 512, LANES)
    return pl.pallas_call(
        _ada_kernel,
        grid=(n // tn,),
        in_specs=[pl.BlockSpec((d, 1), lambda j: (0, 0)),
                  pl.BlockSpec((d, tn), lambda j: (0, j)),
                  pl.BlockSpec((1, tn), lambda j: (0, j))],
        out_specs=pl.BlockSpec((1, tn), lambda j: (0, j)),
        out_shape=jax.ShapeDtypeStruct((1, n), F32),
        compiler_params=_params(("parallel",)),
        name="ada_matvec",
    )(c_col, w, b_row)


def _modulate_kernel(x_ref, sc_ref, sh_ref, o_ref):
    o_ref[...] = (x_ref[...] * (1.0 + sc_ref[...]) + sh_ref[...]).astype(o_ref.dtype)


def _modulate(x, scale, shift):
    s, d = x.shape
    tm = _pick(s, 512)
    row = pl.BlockSpec((1, d), lambda i: (0, 0))
    return pl.pallas_call(
        _modulate_kernel,
        grid=(s // tm,),
        in_specs=[pl.BlockSpec((tm, d), lambda i: (i, 0)), row, row],
        out_specs=pl.BlockSpec((tm, d), lambda i: (i, 0)),
        out_shape=jax.ShapeDtypeStruct((s, d), BF16),
        compiler_params=_params(("parallel",)),
        name="modulate",
    )(x, scale, shift)


def _mm_kernel(a_ref, w_ref, *rest, has_scale, act):
    o_ref = rest[-1]
    acc = jnp.dot(a_ref[...], w_ref[...], preferred_element_type=F32)
    if has_scale:
        acc = acc * rest[0][...]
    if act == "sigmoid":
        acc = jax.nn.sigmoid(acc)
    o_ref[...] = acc.astype(o_ref.dtype)


def _matmul(a, w, out_dtype, col_scale=None, act=None, tm_pref=1024, tn_pref=512, name="matmul"):
    m, k = a.shape
    n = w.shape[1]
    tm = _pick(m, tm_pref)
    tn = _pick(n, tn_pref, LANES)
    in_specs = [pl.BlockSpec((tm, k), lambda i, j: (i, 0)),
                pl.BlockSpec((k, tn), lambda i, j: (0, j))]
    args = [a, w]
    if col_scale is not None:
        in_specs.append(pl.BlockSpec((1, tn), lambda i, j: (0, j)))
        args.append(col_scale)
    return pl.pallas_call(
        functools.partial(_mm_kernel, has_scale=col_scale is not None, act=act),
        grid=(m // tm, n // tn),
        in_specs=in_specs,
        out_specs=pl.BlockSpec((tm, tn), lambda i, j: (i, j)),
        out_shape=jax.ShapeDtypeStruct((m, n), out_dtype),
        compiler_params=_params(("parallel", "arbitrary")),
        name=name,
    )(*args)


def _merge_kernel(att_ref, rw_ref, wa_ref, wr_ref, ga_ref, gr_ref, o_ref):
    ya = jnp.dot(att_ref[...], wa_ref[...], preferred_element_type=F32)
    yr = jnp.dot(rw_ref[...], wr_ref[...], preferred_element_type=F32)
    o_ref[...] = (ga_ref[...].astype(F32) * ya + gr_ref[...].astype(F32) * yr).astype(o_ref.dtype)


def _merge(att, rw, w_up_att, w_up_rwkv, gates, d):
    s, ka = att.shape
    kr = rw.shape[1]
    tm = _pick(s, 1024)
    tn = _pick(d, 512, LANES)
    nj = d // tn
    return pl.pallas_call(
        _merge_kernel,
        grid=(s // tm, nj),
        in_specs=[pl.BlockSpec((tm, ka), lambda i, j: (i, 0)),
                  pl.BlockSpec((tm, kr), lambda i, j: (i, 0)),
                  pl.BlockSpec((ka, tn), lambda i, j: (0, j)),
                  pl.BlockSpec((kr, tn), lambda i, j: (0, j)),
                  pl.BlockSpec((tm, tn), lambda i, j: (i, j)),
                  pl.BlockSpec((tm, tn), lambda i, j: (i, j + nj))],
        out_specs=pl.BlockSpec((tm, tn), lambda i, j: (i, j)),
        out_shape=jax.ShapeDtypeStruct((s, d), BF16),
        compiler_params=_params(("parallel", "arbitrary")),
        name="branch_merge",
    )(att, rw, w_up_att, w_up_rwkv, gates, gates)


def _cumsum_kernel(f_ref, b_ref, o_ref, *, n_chunks):
    h = f_ref.shape[0]
    row = lax.broadcasted_iota(jnp.int32, (LANES, LANES), 0)
    col = lax.broadcasted_iota(jnp.int32, (LANES, LANES), 1)
    upper = (row <= col).astype(BF16)
    bias = b_ref[...]

    def body(c, carry):
        off = pl.multiple_of(c * LANES, LANES)
        z = f_ref[:, pl.ds(off, LANES)] + bias
        log_f = -_softplus(-z)
        hi = log_f.astype(BF16)
        r1 = log_f - hi.astype(F32)
        mid = r1.astype(BF16)
        lo = (r1 - mid.astype(F32)).astype(BF16)
        cs = (jnp.dot(hi, upper, preferred_element_type=F32)
              + jnp.dot(mid, upper, preferred_element_type=F32)
              + jnp.dot(lo, upper, preferred_element_type=F32)) + carry
        o_ref[:, pl.ds(off, LANES)] = cs
        return cs[:, LANES - 1:LANES]

    lax.fori_loop(0, n_chunks, body, jnp.zeros((h, 1), F32))


def _forget_cumsum(f_t, bias_col):
    h, s = f_t.shape
    assert s % LANES == 0
    return pl.pallas_call(
        functools.partial(_cumsum_kernel, n_chunks=s // LANES),
        grid=(1,),
        in_specs=[pl.BlockSpec((h, s), lambda i: (0, 0)), pl.BlockSpec((h, 1), lambda i: (0, 0))],
        out_specs=pl.BlockSpec((h, s), lambda i: (0, 0)),
        out_shape=jax.ShapeDtypeStruct((h, s), F32),
        compiler_params=_params(("arbitrary",)),
        name="forget_cumsum",
    )(f_t, bias_col)


LOG2E = 1.4426950408889634
FOX_BLOCK = 1024


def _fox_kernel(q_ref, k_ref, v_ref, ck_ref, o_ref, m_sc, l_sc, acc_sc, *, tq):
    i = pl.program_id(1)
    nsub = tq // LANES
    m_sc[...] = jnp.full(m_sc.shape, NEG_INF, F32)
    l_sc[...] = jnp.zeros(l_sc.shape, F32)
    acc_sc[...] = jnp.zeros(acc_sc.shape, F32)
    q = q_ref[...]
    cq_parts = []
    for rblk in range(nsub):
        roff = pl.multiple_of(i * tq + rblk * LANES, LANES)
        rowv = ck_ref[:, pl.ds(roff, LANES)] * LOG2E
        cq_parts.append(jnp.transpose(jnp.broadcast_to(rowv, (LANES, LANES))))
    cq = jnp.concatenate(cq_parts, axis=0)

    def step(j, masked):
        off = pl.multiple_of(j * tq, tq)
        k = k_ref[pl.ds(off, tq), :]
        v = v_ref[pl.ds(off, tq), :]
        s = lax.dot_general(q, k, (((1,), (1,)), ((), ())), preferred_element_type=F32)
        ck = ck_ref[:, pl.ds(off, tq)] * LOG2E
        ts = []
        for c in range(nsub):
            t = s[:, c * LANES:(c + 1) * LANES] - ck[:, c * LANES:(c + 1) * LANES]
            if masked:
                r = lax.broadcasted_iota(jnp.int32, (tq, LANES), 0)
                cc = lax.broadcasted_iota(jnp.int32, (tq, LANES), 1) + c * LANES
                t = jnp.where(cc <= r, t, NEG_INF)
            ts.append(t)
        mloc = ts[0]
        for c in range(1, nsub):
            mloc = jnp.maximum(mloc, ts[c])
        m_prev = m_sc[...]
        m_new = jnp.maximum(m_prev, jnp.max(mloc, axis=1, keepdims=True) + cq)
        alpha = jnp.exp2(m_prev - m_new)
        shift = m_new - cq
        ps = [jnp.exp2(t - shift) for t in ts]
        lsum = ps[0]
        for c in range(1, nsub):
            lsum = lsum + ps[c]
        p = jnp.concatenate([x.astype(v.dtype) for x in ps], axis=1)
        l_sc[...] = alpha * l_sc[...] + lsum
        acc_sc[...] = alpha * acc_sc[...] + jnp.dot(p, v, preferred_element_type=F32)
        m_sc[...] = m_new

    def body(j, carry):
        step(j, False)
        return carry

    lax.fori_loop(0, i, body, 0)
    step(i, True)
    l_tot = jnp.sum(l_sc[...], axis=1, keepdims=True)
    o_ref[...] = (acc_sc[...] / l_tot).astype(o_ref.dtype)


def _fox_attention(qkv, cum_k, n_heads, dh):
    s = qkv.shape[0]
    assert dh == LANES
    tq = _pick(s, FOX_BLOCK, LANES)
    nq = s // tq
    return pl.pallas_call(
        functools.partial(_fox_kernel, tq=tq),
        grid=(n_heads, nq),
        in_specs=[pl.BlockSpec((tq, dh), lambda h, i: (i, h)),
                  pl.BlockSpec((s, dh), lambda h, i: (0, n_heads + h)),
                  pl.BlockSpec((s, dh), lambda h, i: (0, 2 * n_heads + h)),
                  pl.BlockSpec((None, 1, s), lambda h, i: (h, 0, 0))],
        out_specs=pl.BlockSpec((tq, dh), lambda h, i: (i, h)),
        out_shape=jax.ShapeDtypeStruct((s, n_heads * dh), BF16),
        scratch_shapes=[pltpu.VMEM((tq, LANES), F32), pltpu.VMEM((tq, LANES), F32), pltpu.VMEM((tq, dh), F32)],
        compiler_params=_params(("parallel", "arbitrary")),
        name="fox_attention",
    )(qkv, qkv, qkv, cum_k)


def _shifted(p_ref, pp_ref, mu_ref, first):
    p = p_ref[...]
    rows = p.shape[0]
    prev_last = jnp.where(first, 0.0, pp_ref[7:8, :])
    rolled = pltpu.roll(p, 1, 0)
    r = lax.broadcasted_iota(jnp.int32, (rows, 1), 0)
    p_prev = jnp.where(r == 0, prev_last, rolled)
    return p + (p_prev - p) * mu_ref[...]


INV_LEAF = 8


def _unit_lower_inverse(lm, sr, sc, n, eye):
    same = lambda b: (sr // b) == (sc // b)
    leaf = min(INV_LEAF, n)
    l0 = jnp.where(same(leaf), lm, 0.0)
    inv = eye - l0
    pw = l0
    span = 2
    while span < leaf:
        pw = _bmm(pw, pw)
        inv = inv + _bmm(inv, pw)
        span *= 2
    b = leaf
    while b < n:
        off = jnp.where(same(2 * b) & jnp.logical_not(same(b)), lm, 0.0)
        inv = inv - _bmm(inv, _bmm(off, inv))
        b *= 2
    return inv


def _bmm(a, b):
    return lax.dot_general(a.astype(BF16), b.astype(BF16), (((2,), (1,)), ((0,), (0,))),
                           preferred_element_type=F32)


def _bmm_nt(a, b):
    return lax.dot_general(a.astype(BF16), b.astype(BF16), (((2,), (2,)), ((0,), (0,))),
                           preferred_element_type=F32)


def _rwkv_kernel(pr_ref, pk_ref, pv_ref, pl_ref, ppr_ref, ppk_ref, ppv_ref, ppl_ref,
                 mur_ref, muk_ref, muv_ref, mul_ref,
                 w0_ref, w2_ref, a0_ref, a2_ref, g2_ref, kk_ref, ka_ref, rk_ref, gng_ref, gnb_ref,
                 o_ref, state_sc, *, chunk, n_chunks, head_dim, dr, ir):
    i = pl.program_id(1)
    first = i == 0

    @pl.when(first)
    def _():
        state_sc[...] = jnp.zeros(state_sc.shape, F32)

    r = _shifted(pr_ref, ppr_ref, mur_ref, first)
    k = _shifted(pk_ref, ppk_ref, muk_ref, first)
    v = _shifted(pv_ref, ppv_ref, muv_ref, first)
    low = _shifted(pl_ref, ppl_ref, mul_ref, first)
    wd = low[:, :dr]
    ad = low[:, dr:dr + ir]
    gd = low[:, dr + ir:]

    z = w0_ref[...] + _bdot(jnp.tanh(wd), w2_ref[...])
    log_w = -jnp.exp(-_softplus(-z) - 0.5)
    a = jax.nn.sigmoid(a0_ref[...] + _bdot(ad, a2_ref[...]))
    g = _bdot(jax.nn.sigmoid(gd), g2_ref[...])

    lane_r = lax.broadcasted_iota(jnp.int32, (LANES, LANES), 0)
    lane_c = lax.broadcasted_iota(jnp.int32, (LANES, LANES), 1)
    same_head = (lane_r // head_dim) == (lane_c // head_dim)
    ones_bd = same_head.astype(BF16)

    kk_raw = k * kk_ref[...]
    ss = _dot_hl(kk_raw * kk_raw, ones_bd)
    kappa = kk_raw / jnp.maximum(jnp.sqrt(ss), 1e-12)
    k_mod = k * (1.0 + (a - 1.0) * ka_ref[...])
    beta = kappa * a

    c2 = 2 * chunk
    lane = lax.broadcasted_iota(jnp.int32, (1, LANES), 1)
    head0 = lane < head_dim
    tr = lax.broadcasted_iota(jnp.int32, (chunk, chunk), 0)
    tc = lax.broadcasted_iota(jnp.int32, (chunk, chunk), 1)
    tri = (tc <= tr).astype(BF16)
    sr = lax.broadcasted_iota(jnp.int32, (c2, c2), 0)
    sc = lax.broadcasted_iota(jnp.int32, (c2, c2), 1)
    same_blk = (sr // chunk) == (sc // chunk)
    strict = same_blk & ((sc % chunk) < (sr % chunk))
    incl = same_blk & ((sc % chunk) <= (sr % chunk))
    eye = (sr == sc).astype(F32)

    ng = n_chunks
    tm = ng * chunk

    def chunked(x):
        return x.reshape(ng, chunk, LANES)

    def stack(x):
        return jnp.concatenate([jnp.where(head0, x, 0.0), jnp.where(head0, 0.0, x)], axis=1)

    lw = chunked(log_w)
    tri_b = jnp.broadcast_to(tri, (ng, chunk, chunk))
    hi = lw.astype(BF16)
    r1 = lw - hi.astype(F32)
    mid = r1.astype(BF16)
    lo = (r1 - mid.astype(F32)).astype(BF16)
    cum = _bmm(tri_b, hi) + _bmm(tri_b, mid) + _bmm(tri_b, lo)
    e_inc = jnp.exp(cum)
    e_neg = jnp.exp(-cum)
    e_exc = jnp.exp(cum - lw)
    rs = stack(chunked(r) * e_inc)
    ks = stack(chunked(k_mod) * e_neg)
    bs = stack(chunked(beta) * e_neg)
    kps = stack(chunked(kappa) * e_exc)
    vs = stack(chunked(v))
    prod = _bmm_nt(jnp.concatenate([kps, rs], axis=1), jnp.concatenate([ks, bs], axis=1))
    a_kk = jnp.where(strict, prod[:, :c2, :c2], 0.0)
    lm = jnp.where(strict, prod[:, :c2, c2:], 0.0)
    a_rk = jnp.where(incl, prod[:, c2:, :c2], 0.0)
    a_rb = jnp.where(incl, prod[:, c2:, c2:], 0.0)
    tinv = _unit_lower_inverse(lm, sr, sc, chunk, eye)
    av = _bmm(jnp.concatenate([a_kk, a_rk], axis=1), vs)
    wu = _bmm(tinv, jnp.concatenate([kps, av[:, :c2]], axis=2))
    rbw = _bmm(a_rb, wu)
    qe = rs - rbw[:, :, :LANES]
    y0 = av[:, c2:] - rbw[:, :, LANES:]
    mp, g0 = [], []
    for c in range(ng):
        tb = _bdot_tn(wu[c], bs[c])
        mp.append(tb[:LANES])
        g0.append(_bdot_tn(vs[c], ks[c]) - tb[LANES:])

    ys = []
    state = state_sc[...]
    for c in range(ng):
        y_st = _bdot_nt(qe[c], state) + y0[c]
        ys.append(y_st[:chunk] + y_st[chunk:])
        state = (state - _bdot(state, mp[c]) + g0[c]) * e_inc[c, chunk - 1:chunk, :]
    state_sc[...] = state
    y = jnp.concatenate(ys, axis=0) if ng > 1 else ys[0]
    assert y.shape == (tm, LANES)

    avg_bd = ones_bd * (1.0 / head_dim)
    mean = _dot_hl(y, avg_bd)
    dlt = y - mean
    var = _dot_hl(dlt * dlt, avg_bd)
    yn = dlt * lax.rsqrt(var + RWKV_GN_EPS) * gng_ref[...] + gnb_ref[...]
    bonus = _dot_hl(r * k_mod * rk_ref[...], ones_bd) * v
    o_ref[...] = ((yn + bonus) * g).astype(o_ref.dtype)


def _rwkv_mix(p, mu, w0, w2, a0, a2, g2, k_k, k_a, r_k, gn_g, gn_b, head_dim):
    s, rc = p.shape
    cw = w0.shape[1]
    dr, ir, gr = w2.shape[0], a2.shape[0], g2.shape[0]
    lw = dr + ir + gr
    assert head_dim * 2 == LANES and cw % LANES == 0 and (head_dim & (head_dim - 1)) == 0
    assert (3 * cw) % lw == 0 and rc == 3 * cw + lw
    chunk = min(RWKV_CHUNK, s)
    tm = _pick(s, RWKV_TILE_CHUNKS * chunk, chunk)
    n_chunks = tm // chunk
    npair = cw // LANES
    low_blk = (3 * cw) // lw
    t8 = tm // 8

    def prev(col_fn):
        return lambda h, i: (jnp.maximum(i * t8 - 1, 0), col_fn(h))

    main = lambda off: pl.BlockSpec((tm, LANES), lambda h, i: (i, off + h))
    prv = lambda off: pl.BlockSpec((8, LANES), prev(lambda h: off + h))
    vec = lambda off: pl.BlockSpec((1, LANES), lambda h, i: (0, off + h))
    in_specs = [
        main(0), main(npair), main(2 * npair), pl.BlockSpec((tm, lw), lambda h, i: (i, low_blk)),
        prv(0), prv(npair), prv(2 * npair), pl.BlockSpec((8, lw), prev(lambda h: low_blk)),
        vec(0), vec(npair), vec(2 * npair), pl.BlockSpec((1, lw), lambda h, i: (0, low_blk)),
        vec(0), pl.BlockSpec((dr, LANES), lambda h, i: (0, h)),
        vec(0), pl.BlockSpec((ir, LANES), lambda h, i: (0, h)),
        pl.BlockSpec((gr, LANES), lambda h, i: (0, h)),
        vec(0), vec(0), vec(0), vec(0), vec(0),
    ]
    return pl.pallas_call(
        functools.partial(_rwkv_kernel, chunk=chunk, n_chunks=n_chunks, head_dim=head_dim, dr=dr, ir=ir),
        grid=(npair, s // tm),
        in_specs=in_specs,
        out_specs=pl.BlockSpec((tm, LANES), lambda h, i: (i, h)),
        out_shape=jax.ShapeDtypeStruct((s, cw), BF16),
        scratch_shapes=[pltpu.VMEM((LANES, LANES), F32)],
        compiler_params=_params(("parallel", "arbitrary")),
        name="rwkv7_mix",
    )(p, p, p, p, p, p, p, p, mu, mu, mu, mu, w0, w2, a0, a2, g2, k_k, k_a, r_k, gn_g, gn_b)


def _layer_norm_rows(z, g, b):
    mu = jnp.mean(z, axis=-1, keepdims=True)
    zc = z - mu
    var = jnp.mean(zc * zc, axis=-1, keepdims=True)
    return zc * lax.rsqrt(var + LN_EPS) * g + b


def _ln1_kernel(x_ref, mix_ref, gate_ref, g_ref, b_ref, sc_ref, sh_ref, rwh_ref, rwl_ref, rb_ref,
                x1_ref, h_ref, lg_ref, *, alpha):
    z = alpha * x_ref[...] + gate_ref[...] * mix_ref[...].astype(F32)
    x1 = _layer_norm_rows(z, g_ref[...], b_ref[...])
    x1_ref[...] = x1
    h = x1 * (1.0 + sc_ref[...]) + sh_ref[...]
    tm, d = h.shape
    nc = d // LANES
    for c in range(nc):
        h_ref[pl.ds(c, tm, stride=nc), :] = h[:, c * LANES:(c + 1) * LANES]
    hi, lo = _split2(h)
    wh = rwh_ref[...]
    wl = rwl_ref[...]
    lg_ref[...] = (jnp.dot(hi, wh, preferred_element_type=F32) + jnp.dot(lo, wh, preferred_element_type=F32)
                   + jnp.dot(hi, wl, preferred_element_type=F32)) + rb_ref[...]


def _ln1_router(x, mix, gate, g, b, scale, shift, rw_hi, rw_lo, rb, alpha):
    s, d = x.shape
    ne = rw_hi.shape[1]
    tm = _pick(s, 256)
    row = pl.BlockSpec((1, d), lambda i: (0, 0))
    tile = pl.BlockSpec((tm, d), lambda i: (i, 0))
    wspec = pl.BlockSpec((d, ne), lambda i: (0, 0))
    return pl.pallas_call(
        functools.partial(_ln1_kernel, alpha=alpha),
        grid=(s // tm,),
        in_specs=[tile, tile, row, row, row, row, row, wspec, wspec, pl.BlockSpec((1, ne), lambda i: (0, 0))],
        out_specs=[tile, pl.BlockSpec((tm * (d // LANES), LANES), lambda i: (i, 0)),
                   pl.BlockSpec((tm, ne), lambda i: (i, 0))],
        out_shape=[jax.ShapeDtypeStruct((s, d), F32), jax.ShapeDtypeStruct((s * (d // LANES), LANES), F32),
                   jax.ShapeDtypeStruct((s, ne), F32)],
        compiler_params=_params(("parallel",)),
        name="ln1_router",
    )(x, mix, gate, g, b, scale, shift, rw_hi, rw_lo, rb)


def _slab_pitch(nc):
    p = -(-nc // SUBLANES)
    return SUBLANES * (p if p % 2 else p + 1)


def _gather_kernel(tok_ref, base_ref, na_ref, h_hbm, o_ref, buf, sem, *, tm, nc, pitch, unroll):
    b = pl.program_id(0)
    base = base_ref[b]
    last = tok_ref.shape[0] - 1

    def slab_copy(r, tok):
        src = h_hbm.at[pl.ds(pl.multiple_of(tok * nc, SUBLANES), nc)]
        dst = buf.at[pl.ds(pl.multiple_of(r * pitch, SUBLANES), nc)]
        return pltpu.make_async_copy(src, dst, sem)

    def start(r0, carry):
        for u in range(unroll):
            r = r0 * unroll + u
            slab_copy(r, tok_ref[jnp.minimum(base + r, last)]).start(priority=u % 2)
        return carry

    def wait(r0, carry):
        for u in range(unroll):
            slab_copy(r0 * unroll + u, 0).wait()
        return carry

    @pl.when(b < na_ref[0])
    def _():
        lax.fori_loop(0, tm // unroll, start, 0)
        lax.fori_loop(0, tm // unroll, wait, 0)
        for c in range(nc):
            o_ref[:, c * LANES:(c + 1) * LANES] = buf[pl.ds(c, tm, stride=pitch), :].astype(o_ref.dtype)

    @pl.when(b >= na_ref[0])
    def _():
        o_ref[...] = jnp.zeros(o_ref.shape, o_ref.dtype)


def _gather_rows(h_slab, sorted_tok, src_base, n_active, tm, d):
    n_blocks = src_base.shape[0]
    nc = d // LANES
    assert nc % SUBLANES == 0 and h_slab.shape[1] == LANES
    pitch = _slab_pitch(nc)
    return pl.pallas_call(
        functools.partial(_gather_kernel, tm=tm, nc=nc, pitch=pitch, unroll=8),
        grid_spec=pltpu.PrefetchScalarGridSpec(
            num_scalar_prefetch=3,
            grid=(n_blocks,),
            in_specs=[pl.BlockSpec(memory_space=pl.ANY)],
            out_specs=pl.BlockSpec((tm, d), lambda b, tok, base, na: (b, 0)),
            scratch_shapes=[pltpu.VMEM((tm * pitch, LANES), h_slab.dtype), pltpu.SemaphoreType.DMA(())]),
        out_shape=jax.ShapeDtypeStruct((n_blocks * tm, d), BF16),
        compiler_params=_params(("arbitrary",)),
        name="moe_gather",
    )(sorted_tok, src_base, n_active, h_slab)


def _tile_schedule(block_e, blk_start, nblk, n_active, n_tiles):
    i32 = jnp.int32
    n_steps = block_e.shape[0] * n_tiles
    s = jnp.arange(n_steps, dtype=i32)
    b0 = s // n_tiles
    live = b0 < n_active[0]
    e = block_e[b0]
    nb_e = jnp.maximum(nblk[e], 1)
    local = s - n_tiles * blk_start[e]
    e_last = block_e[jnp.maximum(n_active[0] - 1, 0)]
    sb = jnp.where(live, blk_start[e] + local % nb_e, b0)
    st_out = jnp.where(live, local // nb_e, s % n_tiles)
    st_w = jnp.where(live, local // nb_e, n_tiles - 1)
    se = jnp.where(live, e, e_last)
    key = se * n_tiles + st_w
    fresh = jnp.concatenate([jnp.ones((1,), i32), (key[1:] != key[:-1]).astype(i32)])
    n_live = (n_active * n_tiles).astype(i32)
    return sb.astype(i32), st_out.astype(i32), st_w.astype(i32), se.astype(i32), fresh, n_live


def _gate_up_kernel(sb, so, sw, se, fresh, n_live, x_ref, wg_ref, wu_ref, bg_ref, bu_ref, o_ref, wg_sc, wu_sc):
    s = pl.program_id(0)

    @pl.when(fresh[s] == 1)
    def _():
        wg_sc[...] = wg_ref[...].astype(BF16)
        wu_sc[...] = wu_ref[...].astype(BF16)

    @pl.when(s < n_live[0])
    def _():
        x = x_ref[...]
        gate = jnp.dot(x, wg_sc[...], preferred_element_type=F32) + bg_ref[...]
        up = jnp.dot(x, wu_sc[...], preferred_element_type=F32) + bu_ref[...]
        gate = jnp.minimum(gate, SWIGLU_LIMIT)
        up = jnp.clip(up, -SWIGLU_LIMIT, SWIGLU_LIMIT)
        o_ref[...] = ((up + 1.0) * (gate * jax.nn.sigmoid(gate * SWIGLU_ALPHA))).astype(o_ref.dtype)

    @pl.when(s >= n_live[0])
    def _():
        o_ref[...] = jnp.zeros(o_ref.shape, o_ref.dtype)


def _down_kernel(sb, so, sw, se, fresh, n_live, a_ref, wd_ref, bd_ref, o_ref, wd_sc):
    s = pl.program_id(0)

    @pl.when(fresh[s] == 1)
    def _():
        wd_sc[...] = wd_ref[...].astype(BF16)

    @pl.when(s < n_live[0])
    def _():
        o_ref[...] = jnp.dot(a_ref[...], wd_sc[...], preferred_element_type=F32) + bd_ref[...]

    @pl.when(s >= n_live[0])
    def _():
        o_ref[...] = jnp.zeros(o_ref.shape, o_ref.dtype)


def _expert_ffn(xg, block_e, blk_start, nblk, n_active, w_gu, b_gu, w_down, b_down, tm):
    p, d = xg.shape
    fdim = w_down.shape[1]
    n_blocks = p // tm
    tf = _pick(fdim, 512, LANES)
    nf = fdim // tf
    sched = _tile_schedule(block_e, blk_start, nblk, n_active, nf)
    act = pl.pallas_call(
        _gate_up_kernel,
        grid_spec=pltpu.PrefetchScalarGridSpec(
            num_scalar_prefetch=6,
            grid=(n_blocks * nf,),
            in_specs=[
                pl.BlockSpec((tm, d), lambda s, sb, so, sw, se, fr, nl: (sb[s], 0)),
                pl.BlockSpec((None, d, tf), lambda s, sb, so, sw, se, fr, nl: (se[s], 0, sw[s])),
                pl.BlockSpec((None, d, tf), lambda s, sb, so, sw, se, fr, nl: (se[s], 0, nf + sw[s])),
                pl.BlockSpec((None, 1, tf), lambda s, sb, so, sw, se, fr, nl: (se[s], 0, sw[s])),
                pl.BlockSpec((None, 1, tf), lambda s, sb, so, sw, se, fr, nl: (se[s], 0, nf + sw[s])),
            ],
            out_specs=pl.BlockSpec((tm, tf), lambda s, sb, so, sw, se, fr, nl: (sb[s], so[s])),
            scratch_shapes=[pltpu.VMEM((d, tf), BF16), pltpu.VMEM((d, tf), BF16)]),
        out_shape=jax.ShapeDtypeStruct((p, fdim), BF16),
        compiler_params=_params(("arbitrary",)),
        name="moe_gate_up",
    )(*sched, xg, w_gu, w_gu, b_gu, b_gu)

    tn = _pick(d, 1024, LANES)
    nn = d // tn
    sched = _tile_schedule(block_e, blk_start, nblk, n_active, nn)
    return pl.pallas_call(
        _down_kernel,
        grid_spec=pltpu.PrefetchScalarGridSpec(
            num_scalar_prefetch=6,
            grid=(n_blocks * nn,),
            in_specs=[
                pl.BlockSpec((tm, fdim), lambda s, sb, so, sw, se, fr, nl: (sb[s], 0)),
                pl.BlockSpec((None, fdim, tn), lambda s, sb, so, sw, se, fr, nl: (se[s], 0, sw[s])),
                pl.BlockSpec((None, 1, tn), lambda s, sb, so, sw, se, fr, nl: (se[s], 0, sw[s])),
            ],
            out_specs=pl.BlockSpec((tm, tn), lambda s, sb, so, sw, se, fr, nl: (sb[s], so[s])),
            scratch_shapes=[pltpu.VMEM((fdim, tn), BF16)]),
        out_shape=jax.ShapeDtypeStruct((p, d), F32),
        compiler_params=_params(("arbitrary",)),
        name="moe_down",
    )(*sched, act, w_down, b_down)


def _combine_kernel(pos_ref, y_hbm, x1_ref, gw_ref, gate_ref, g_ref, b_ref, o_ref, buf, sem, *, tm, alpha):
    base = pl.program_id(0) * tm * TOP_K
    unroll = 2

    def row_copy(r, kk, src):
        return pltpu.make_async_copy(y_hbm.at[pl.ds(src, 1)], buf.at[kk, pl.ds(r, 1)], sem)

    def start(r0, carry):
        for u in range(unroll):
            r = r0 * unroll + u
            for kk in range(TOP_K):
                row_copy(r, kk, pos_ref[base + r * TOP_K + kk]).start(priority=kk % 2)
        return carry

    def wait(r0, carry):
        for u in range(unroll):
            for kk in range(TOP_K):
                row_copy(r0 * unroll + u, kk, 0).wait()
        return carry

    lax.fori_loop(0, tm // unroll, start, 0)
    lax.fori_loop(0, tm // unroll, wait, 0)
    gw = gw_ref[...]
    ff = buf[0] * gw[:, 0:1]
    for kk in range(1, TOP_K):
        ff = ff + buf[kk] * gw[:, kk:kk + 1]
    z = alpha * x1_ref[...] + gate_ref[...] * ff
    o_ref[...] = _layer_norm_rows(z, g_ref[...], b_ref[...])


def _combine_ln2(y, pos, x1, gate_w, gate, g, b, alpha):
    s, d = x1.shape
    tm = _pick(s, 128)
    row = pl.BlockSpec((1, d), lambda i, pos: (0, 0))
    tile = pl.BlockSpec((tm, d), lambda i, pos: (i, 0))
    return pl.pallas_call(
        functools.partial(_combine_kernel, tm=tm, alpha=alpha),
        grid_spec=pltpu.PrefetchScalarGridSpec(
            num_scalar_prefetch=1,
            grid=(s // tm,),
            in_specs=[pl.BlockSpec(memory_space=pl.ANY), tile,
                      pl.BlockSpec((tm, TOP_K), lambda i, pos: (i, 0)), row, row, row],
            out_specs=tile,
            scratch_shapes=[pltpu.VMEM((TOP_K, tm, d), F32), pltpu.SemaphoreType.DMA(())]),
        out_shape=jax.ShapeDtypeStruct((s, d), F32),
        compiler_params=_params(("arbitrary",)),
        name="moe_combine_ln2",
    )(pos, y, x1, gate_w, gate, g, b)


def _routing_plan(logits, n_experts, tm):
    t = logits.shape[0]
    i32 = jnp.int32
    top_vals, top_idx = lax.top_k(logits, TOP_K)
    gate_w = jax.nn.softmax(top_vals, axis=-1)
    tk = t * TOP_K
    flat_e = top_idx.reshape(tk).astype(i32)
    iota = jnp.arange(tk, dtype=i32)
    sorted_e, order = lax.sort((flat_e, iota), num_keys=1, is_stable=True)
    experts = jnp.arange(n_experts, dtype=i32)
    counts = jnp.sum((flat_e[:, None] == experts[None, :]).astype(i32), axis=0)
    padded = ((counts + tm - 1) // tm) * tm
    start = jnp.cumsum(counts) - counts
    pad_end = jnp.cumsum(padded)
    pad_start = pad_end - padded
    shift_e = pad_start - start
    dest = iota + jnp.sum(jnp.where(sorted_e[:, None] == experts[None, :], shift_e[None, :], 0), axis=1)
    _, pos = lax.sort((order, dest), num_keys=1)
    n_blocks = -(-tk // tm) + n_experts
    blk_row0 = jnp.arange(n_blocks, dtype=i32) * tm
    block_e = jnp.minimum(jnp.sum((pad_end[None, :] <= blk_row0[:, None]).astype(i32), axis=1), n_experts - 1)
    src_base = blk_row0 - shift_e[block_e]
    n_active = (pad_end[-1:] // tm).astype(i32)
    blocks = (block_e.astype(i32), (pad_start // tm).astype(i32), (padded // tm).astype(i32), n_active)
    return gate_w, (order // TOP_K).astype(i32), src_base.astype(i32), pos, blocks


def _layer(x, c_col, prm):
    s, d = x.shape
    n_att_heads = prm["fox_f_bias"].shape[0]
    att_w = prm["w_up_att"].shape[0]
    dh = att_w // n_att_heads
    n_rw_heads, rw_hd = prm["rwkv_r_k"].shape
    rc = prm["rwkv_mu"].shape[0]
    n_experts = prm["router_w"].shape[1]
    depth_alpha = prm["alpha"]

    ada = _ada(c_col, prm["w_ada"], prm["b_ada"][None, :])
    mods = [ada[:, m * d:(m + 1) * d] for m in range(6)]
    shift_m, scale_m, gate_m, shift_f, scale_f, gate_f = mods

    h = _modulate(x, scale_m, shift_m)
    w_in = prm["w_in"]
    off_f = 3 * att_w
    off_rw = off_f + n_att_heads
    off_gate = off_rw + rc
    w_qkv = w_in[:, :off_f].astype(BF16)
    w_f = jnp.pad(w_in[:, off_f:off_rw], ((0, 0), (0, LANES - n_att_heads))).astype(BF16)
    w_rw = w_in[:, off_rw:off_gate].astype(BF16)
    w_gt = w_in[:, off_gate:].astype(BF16)
    q_scale = jnp.concatenate([jnp.full((1, att_w), dh ** -0.5 * LOG2E, F32), jnp.ones((1, 2 * att_w), F32)],
                              axis=1)
    qkv = _matmul(h, w_qkv, BF16, col_scale=q_scale, name="proj_qkv")
    f_logit = _matmul(h, w_f, F32, name="proj_forget")
    p_rw = _matmul(h, w_rw, F32, name="proj_rwkv")
    gates = _matmul(h, w_gt, BF16, act="sigmoid", name="proj_gates")

    cum = _forget_cumsum(f_logit[:, :n_att_heads].T, prm["fox_f_bias"][:, None])
    att = _fox_attention(qkv, cum[:, None, :], n_att_heads, dh)

    row = lambda v: v.reshape(1, -1)
    rw = _rwkv_mix(p_rw, row(prm["rwkv_mu"]), row(prm["rwkv_w0"]), prm["rwkv_w2"].astype(BF16),
                   row(prm["rwkv_a0"]), prm["rwkv_a2"].astype(BF16), prm["rwkv_g2"].astype(BF16),
                   row(prm["rwkv_k_k"]), row(prm["rwkv_k_a"]), row(prm["rwkv_r_k"]),
                   row(prm["rwkv_gn_g"]), row(prm["rwkv_gn_b"]), rw_hd)

    merged = _merge(att, rw, prm["w_up_att"].astype(BF16), prm["w_up_rwkv"].astype(BF16), gates, d)
    mix = _matmul(merged, prm["w_o"].astype(BF16), F32, name="proj_out")

    ne_pad = -(-n_experts // LANES) * LANES
    rw_full = jnp.pad(prm["router_w"], ((0, 0), (0, ne_pad - n_experts)))
    rw_hi = rw_full.astype(BF16)
    rw_lo = (rw_full - rw_hi.astype(F32)).astype(BF16)
    rb = jnp.pad(prm["router_b"], (0, ne_pad - n_experts))[None, :]
    x1, h2, logits = _ln1_router(x, mix, gate_m, row(prm["ln1_g"]), row(prm["ln1_b"]), scale_f, shift_f,
                                 rw_hi, rw_lo, rb, depth_alpha)

    tm_e = _pick(s, 512)
    gate_w, sorted_tok, src_base, pos, blocks = _routing_plan(logits[:, :n_experts], n_experts, tm_e)
    xg = _gather_rows(h2, sorted_tok, src_base, blocks[3], tm_e, d)
    y = _expert_ffn(xg, *blocks, prm["w_gate_up"], prm["b_gate_up"][:, None, :],
                    prm["w_down"], prm["b_down"][:, None, :], tm_e)
    return _combine_ln2(y, pos, x1, gate_w, gate_f, row(prm["ln2_g"]), row(prm["ln2_b"]), depth_alpha)


def kernel(x, c, w_ada, b_ada, w_in, fox_f_bias, rwkv_mu, rwkv_w0, rwkv_w2, rwkv_a0, rwkv_a2, rwkv_g2,
           rwkv_k_k, rwkv_k_a, rwkv_r_k, rwkv_gn_g, rwkv_gn_b, w_up_att, w_up_rwkv, w_o, ln1_g, ln1_b,
           router_w, router_b, w_gate_up, b_gate_up, w_down, b_down, ln2_g, ln2_b):
    stacked = dict(w_ada=w_ada, b_ada=b_ada, w_in=w_in, fox_f_bias=fox_f_bias, rwkv_mu=rwkv_mu,
                   rwkv_w0=rwkv_w0, rwkv_w2=rwkv_w2, rwkv_a0=rwkv_a0, rwkv_a2=rwkv_a2, rwkv_g2=rwkv_g2,
                   rwkv_k_k=rwkv_k_k, rwkv_k_a=rwkv_k_a, rwkv_r_k=rwkv_r_k, rwkv_gn_g=rwkv_gn_g,
                   rwkv_gn_b=rwkv_gn_b, w_up_att=w_up_att, w_up_rwkv=w_up_rwkv, w_o=w_o, ln1_g=ln1_g,
                   ln1_b=ln1_b, router_w=router_w, router_b=router_b, w_gate_up=w_gate_up,
                   b_gate_up=b_gate_up, w_down=w_down, b_down=b_down, ln2_g=ln2_g, ln2_b=ln2_b)
    depth = w_ada.shape[0]
    alpha = (2 * depth) ** 0.25
    outs = []
    n_batch, seq, d_model = x.shape
    for bi in range(n_batch):
        xb = x.reshape(seq, d_model) if n_batch == 1 else x[bi]
        c_col = c[bi][:, None]
        for layer in range(depth):
            prm = {name: val[layer] for name, val in stacked.items()}
            prm["alpha"] = alpha
            xb = _layer(xb, c_col, prm)
        outs.append(xb)
    if n_batch == 1:
        return outs[0].reshape(1, seq, d_model)
    return jnp.stack(outs, axis=0)
```
